```python
import jax, jax.numpy as jnp
from jax import lax
import numpy as np

D_MODEL = 4096
BATCH = 4
SEQ = 4096
DEPTH = 1

HEAD_DIM = 128
D_MIX = D_MODEL
W_A = D_MIX // 2
W_B = D_MIX - W_A
N_GROUPS_A = W_A // HEAD_DIM
N_GROUPS_B = W_B // HEAD_DIM
CONV_A = 3
CONV_B = 31
D_IN = 4 * W_A + 3 * W_B
SPLITS = (W_A, 2 * W_A, 3 * W_A, 4 * W_A, 4 * W_A + W_B, 4 * W_A + 2 * W_B)
EPS = 1e-6

kernel_name = "hymba_style_conv_hybrid_adaln"


def _rmsnorm(x, g):
    xf = x.astype(jnp.float32)
    y = xf * lax.rsqrt(jnp.mean(xf * xf, axis=-1, keepdims=True) + EPS)
    return (y * g.astype(jnp.float32)).astype(x.dtype)


def _layernorm(x, g, b):
    xf = x.astype(jnp.float32)
    mu = jnp.mean(xf, axis=-1, keepdims=True)
    xc = xf - mu
    var = jnp.mean(xc * xc, axis=-1, keepdims=True)
    y = xc * lax.rsqrt(var + EPS) * g.astype(jnp.float32) + b.astype(jnp.float32)
    return y.astype(x.dtype)


def _causal_depthwise_conv(u, w):
    k, ch = w.shape
    return lax.conv_general_dilated(
        u, w[:, None, :].astype(u.dtype),
        window_strides=(1,), padding=((k - 1, 0),),
        dimension_numbers=("NWC", "WIO", "NWC"),
        feature_group_count=ch)


def setup_inputs(seed: int = 0) -> dict:
    key = jax.random.key(seed)
    ks = jax.random.split(key, 14)
    f32 = jnp.float32
    x = jax.random.normal(ks[0], (BATCH, SEQ, D_MODEL), f32)
    c = jax.random.normal(ks[1], (BATCH, D_MODEL), f32)
    norm_g = 1.0 + 0.01 * jax.random.normal(ks[2], (DEPTH, D_MODEL), f32)
    w_ada = 0.5 * D_MODEL ** -0.5 * jax.random.normal(ks[3], (DEPTH, D_MODEL, 3 * D_MODEL), f32)
    b_ada = 0.01 * jax.random.normal(ks[4], (DEPTH, 3 * D_MODEL), f32)
    w_in = D_MODEL ** -0.5 * jax.random.normal(ks[5], (DEPTH, D_MODEL, D_IN), f32)
    conv_a_w = CONV_A ** -0.5 * jax.random.normal(ks[6], (DEPTH, CONV_A, W_A), f32)
    conv_b_w = CONV_B ** -0.5 * jax.random.normal(ks[7], (DEPTH, CONV_B, W_B), f32)
    conv_b_b = 0.01 * jax.random.normal(ks[8], (DEPTH, W_B), f32)
    ln_b_g = 1.0 + 0.01 * jax.random.normal(ks[9], (DEPTH, W_B), f32)
    ln_b_b = 0.01 * jax.random.normal(ks[10], (DEPTH, W_B), f32)
    w_out = D_MIX ** -0.5 * jax.random.normal(ks[11], (DEPTH, D_MIX, D_MODEL), f32)
    final_g = 1.0 + 0.01 * jax.random.normal(ks[12], (D_MODEL,), f32)
    return {"x": x, "c": c, "norm_g": norm_g, "w_ada": w_ada, "b_ada": b_ada,
            "w_in": w_in, "conv_a_w": conv_a_w, "conv_b_w": conv_b_w,
            "conv_b_b": conv_b_b, "ln_b_g": ln_b_g, "ln_b_b": ln_b_b,
            "w_out": w_out, "final_g": final_g}


def reference(x, c, norm_g, w_ada, b_ada, w_in, conv_a_w, conv_b_w, conv_b_b,
              ln_b_g, ln_b_b, w_out, final_g):
    c_act = jax.nn.silu(c)
    for l in range(DEPTH):
        mod = c_act @ w_ada[l] + b_ada[l]
        shift, scale, gate = jnp.split(mod, 3, axis=-1)
        h = _rmsnorm(x, norm_g[l]) * (1.0 + scale[:, None, :]) + shift[:, None, :]

        proj = jnp.einsum("bsd,de->bse", h, w_in[l])
        a_b, a_c, a_x, a_z, b_v, b_g, b_z = jnp.split(proj, SPLITS, axis=-1)

        y_a = a_b * _causal_depthwise_conv(a_c * a_x, conv_a_w[l]) * jax.nn.silu(a_z)

        u = b_v * jax.nn.sigmoid(b_g)
        u = _causal_depthwise_conv(u, conv_b_w[l]) + conv_b_b[l]
        y_b = jax.nn.silu(_layernorm(u, ln_b_g[l], ln_b_b[l])) * jax.nn.silu(b_z)

        y = jnp.concatenate([y_a, y_b], axis=-1)
        x = x + gate[:, None, :] * jnp.einsum("bse,ed->bsd", y, w_out[l])
    return _rmsnorm(x, final_g)
```

```python
import functools

import jax
import jax.numpy as jnp
from jax import lax
from jax.experimental import pallas as pl
from jax.experimental.pallas import tpu as pltpu

EPS = 1e-6
CONV_A = 3
CONV_B = 31
SUBLANES = 8
HALO_A = 8
HALO_B = 32
VMEM_LIMIT = 56 * 1024 * 1024

_dot = functools.partial(jnp.dot, preferred_element_type=jnp.float32)
_bf16 = jnp.bfloat16
_f32 = jnp.float32


def _params(semantics):
    return pltpu.CompilerParams(dimension_semantics=semantics,
                                vmem_limit_bytes=VMEM_LIMIT)


def _ada_kernel(c_ref, w_ref, b_ref, o_ref):
    c_act = jax.nn.silu(c_ref[...]).astype(_bf16)
    o_ref[...] = _dot(c_act, w_ref[...].astype(_bf16)) + b_ref[...]


def _ada(c_pad, w_ada, b_ada, tn):
    m, d = c_pad.shape
    n = w_ada.shape[1]
    return pl.pallas_call(
        _ada_kernel,
        grid=(n // tn,),
        in_specs=[pl.BlockSpec((m, d), lambda j: (0, 0)),
                  pl.BlockSpec((d, tn), lambda j: (0, j)),
                  pl.BlockSpec((1, tn), lambda j: (0, j))],
        out_specs=pl.BlockSpec((m, tn), lambda j: (0, j)),
        out_shape=jax.ShapeDtypeStruct((m, n), _f32),
        compiler_params=_params(("arbitrary",)),
        name="ada",
    )(c_pad, w_ada, b_ada)


def _prenorm_kernel(x_ref, mod_ref, g_ref, h_ref):
    x = x_ref[0]
    ms = jnp.mean(x * x, axis=-1, keepdims=True)
    y = x * lax.rsqrt(ms + EPS) * g_ref[...]
    shift = mod_ref[0, 0:1, :]
    scale = mod_ref[0, 1:2, :]
    h_ref[0] = (y * (1.0 + scale) + shift).astype(_bf16)


def _prenorm(x, mod3, norm_g, ts):
    b, s, d = x.shape
    return pl.pallas_call(
        _prenorm_kernel,
        grid=(b, s // ts),
        in_specs=[pl.BlockSpec((1, ts, d), lambda i, t: (i, t, 0)),
                  pl.BlockSpec((1, 3, d), lambda i, t: (i, 0, 0)),
                  pl.BlockSpec((1, d), lambda i, t: (0, 0))],
        out_specs=pl.BlockSpec((1, ts, d), lambda i, t: (i, t, 0)),
        out_shape=jax.ShapeDtypeStruct((b, s, d), _bf16),
        compiler_params=_params(("arbitrary", "arbitrary")),
        name="prenorm",
    )(x, mod3, norm_g)


def _mixer_a_kernel(h_ref, wb_ref, wc_ref, wx_ref, wz_ref, cw_ref, y_ref,
                    carry_ref, ext_ref, *, ts):
    t = pl.program_id(1)
    j = pl.program_id(2)
    h = h_ref[0]

    @pl.when(t == 0)
    def _():
        carry_ref[j] = jnp.zeros(carry_ref.shape[1:], _f32)

    cx = _dot(h, wc_ref[...]) * _dot(h, wx_ref[...])
    ext_ref[0:HALO_A] = carry_ref[j]
    ext_ref[HALO_A:] = cx
    carry_ref[j] = cx[ts - HALO_A:]
    ext = ext_ref[...]
    w = cw_ref[...]
    conv = w[2:3] * cx
    for d in range(1, CONV_A):
        conv = conv + w[CONV_A - 1 - d:CONV_A - d] * pltpu.roll(ext, d, 0)[HALO_A:]
    y = _dot(h, wb_ref[...]) * conv * jax.nn.silu(_dot(h, wz_ref[...]))
    y_ref[0] = y.astype(_bf16)


def _mixer_a(h, w_in, conv_w, w_a, ts, cw):
    b, s, d = h.shape
    nj = w_a // cw

    def wspec(k):
        return pl.BlockSpec((d, cw), lambda i, t, j, k=k: (0, k * nj + j))

    return pl.pallas_call(
        functools.partial(_mixer_a_kernel, ts=ts),
        grid=(b, s // ts, nj),
        in_specs=[pl.BlockSpec((1, ts, d), lambda i, t, j: (i, t, 0)),
                  wspec(0), wspec(1), wspec(2), wspec(3),
                  pl.BlockSpec((CONV_A, cw), lambda i, t, j: (0, j))],
        out_specs=pl.BlockSpec((1, ts, cw), lambda i, t, j: (i, t, j)),
        out_shape=jax.ShapeDtypeStruct((b, s, w_a), _bf16),
        scratch_shapes=[pltpu.VMEM((nj, HALO_A, cw), _f32),
                        pltpu.VMEM((HALO_A + ts, cw), _f32)],
        compiler_params=_params(("arbitrary", "arbitrary", "arbitrary")),
        name="mixer_a",
    )(h, w_in, w_in, w_in, w_in, conv_w)


def _mixer_b_kernel(h_ref, wv_ref, wg_ref, wz_ref, cw_ref, cb_ref, lg_ref, lb_ref,
                    y_ref, carry_ref, ext_ref, sh_ref, u_ref, z_ref, sum_ref,
                    *, ts, cw, nj):
    t = pl.program_id(1)
    j = pl.program_id(2)
    h = h_ref[0]

    @pl.when(t == 0)
    def _():
        carry_ref[j] = jnp.zeros(carry_ref.shape[1:], _f32)

    @pl.when(j == 0)
    def _():
        sum_ref[...] = jnp.zeros(sum_ref.shape, _f32)

    glu = _dot(h, wv_ref[...]) * jax.nn.sigmoid(_dot(h, wg_ref[...]))
    ext_ref[0:HALO_B] = carry_ref[j]
    ext_ref[HALO_B:] = glu
    carry_ref[j] = glu[ts - HALO_B:]
    ext = ext_ref[...]
    sh_ref[0] = ext
    for r in range(1, SUBLANES):
        sh_ref[r] = pltpu.roll(ext, r, 0)
    w = cw_ref[...]
    conv = jnp.zeros((ts, cw), _f32) + cb_ref[...]
    for d in range(CONV_B):
        a, r = divmod(d, SUBLANES)
        start = HALO_B - SUBLANES * a
        conv = conv + w[CONV_B - 1 - d:CONV_B - d] * sh_ref[r, start:start + ts, :]
    u_ref[j] = conv
    z_ref[j] = jax.nn.silu(_dot(h, wz_ref[...]))
    sum_ref[...] += jnp.sum(conv, axis=-1, keepdims=True)

    @pl.when(j == nj - 1)
    def _():
        width = nj * cw
        mu = sum_ref[...] * (1.0 / width)
        var = jnp.zeros((ts, 1), _f32)
        for k in range(nj):
            xc = u_ref[k] - mu
            var = var + jnp.sum(xc * xc, axis=-1, keepdims=True)
        rstd = lax.rsqrt(var * (1.0 / width) + EPS)
        for k in range(nj):
            cols = slice(k * cw, (k + 1) * cw)
            ln = (u_ref[k] - mu) * rstd * lg_ref[:, cols] + lb_ref[:, cols]
            y_ref[0, :, cols] = (jax.nn.silu(ln) * z_ref[k]).astype(_bf16)


def _mixer_b(h, w_in, conv_w, conv_b, ln_g, ln_b, col0, w_b, ts, cw):
    b, s, d = h.shape
    nj = w_b // cw
    j0 = col0 // cw

    def wspec(k):
        return pl.BlockSpec((d, cw), lambda i, t, j, k=k: (0, j0 + k * nj + j))

    return pl.pallas_call(
        functools.partial(_mixer_b_kernel, ts=ts, cw=cw, nj=nj),
        grid=(b, s // ts, nj),
        in_specs=[pl.BlockSpec((1, ts, d), lambda i, t, j: (i, t, 0)),
                  wspec(0), wspec(1), wspec(2),
                  pl.BlockSpec((CONV_B, cw), lambda i, t, j: (0, j)),
                  pl.BlockSpec((1, cw), lambda i, t, j: (0, j)),
                  pl.BlockSpec((1, w_b), lambda i, t, j: (0, 0)),
                  pl.BlockSpec((1, w_b), lambda i, t, j: (0, 0))],
        out_specs=pl.BlockSpec((1, ts, w_b), lambda i, t, j: (i, t, 0)),
        out_shape=jax.ShapeDtypeStruct((b, s, w_b), _bf16),
        scratch_shapes=[pltpu.VMEM((nj, HALO_B, cw), _f32),
                        pltpu.VMEM((HALO_B + ts, cw), _f32),
                        pltpu.VMEM((SUBLANES, HALO_B + ts, cw), _f32),
                        pltpu.VMEM((nj, ts, cw), _f32),
                        pltpu.VMEM((nj, ts, cw), _f32),
                        pltpu.VMEM((ts, 1), _f32)],
        compiler_params=_params(("arbitrary", "arbitrary", "arbitrary")),
        name="mixer_b",
    )(h, w_in, w_in, w_in, conv_w, conv_b, ln_g, ln_b)


def _outproj_kernel(ya_ref, yb_ref, wa_ref, wb_ref, x_ref, gate_ref, fg_ref, o_ref,
                    r_ref, ss_ref, *, tn, nj):
    j = pl.program_id(2)

    @pl.when(j == 0)
    def _():
        ss_ref[...] = jnp.zeros(ss_ref.shape, _f32)

    delta = _dot(ya_ref[0], wa_ref[...]) + _dot(yb_ref[0], wb_ref[...])
    r = x_ref[0] + gate_ref[0] * delta
    r_ref[j] = r
    ss_ref[...] += jnp.sum(r * r, axis=-1, keepdims=True)

    @pl.when(j == nj - 1)
    def _():
        rstd = lax.rsqrt(ss_ref[...] * (1.0 / (nj * tn)) + EPS)
        for k in range(nj):
            cols = slice(k * tn, (k + 1) * tn)
            o_ref[0, :, cols] = r_ref[k] * rstd * fg_ref[:, cols]


def _outproj(y_a, y_b, w_out, x, gate, final_g, ts, tn):
    b, s, d = x.shape
    w_a = y_a.shape[-1]
    w_b = y_b.shape[-1]
    nj = d // tn
    assert w_a % w_b == 0
    return pl.pallas_call(
        functools.partial(_outproj_kernel, tn=tn, nj=nj),
        grid=(b, s // ts, nj),
        in_specs=[pl.BlockSpec((1, ts, w_a), lambda i, t, j: (i, t, 0)),
                  pl.BlockSpec((1, ts, w_b), lambda i, t, j: (i, t, 0)),
                  pl.BlockSpec((w_a, tn), lambda i, t, j: (0, j)),
                  pl.BlockSpec((w_b, tn), lambda i, t, j: (w_a // w_b, j)),
                  pl.BlockSpec((1, ts, tn), lambda i, t, j: (i, t, j)),
                  pl.BlockSpec((1, 1, tn), lambda i, t, j: (i, 0, j)),
                  pl.BlockSpec((1, d), lambda i, t, j: (0, 0))],
        out_specs=pl.BlockSpec((1, ts, d), lambda i, t, j: (i, t, 0)),
        out_shape=jax.ShapeDtypeStruct((b, s, d), _f32),
        scratch_shapes=[pltpu.VMEM((nj, ts, tn), _f32),
                        pltpu.VMEM((ts, 1), _f32)],
        compiler_params=_params(("arbitrary", "arbitrary", "arbitrary")),
        name="outproj",
    )(y_a, y_b, w_out, w_out, x, gate, final_g)


def kernel(x, c, norm_g, w_ada, b_ada, w_in, conv_a_w, conv_b_w, conv_b_b, ln_b_g,
           ln_b_b, w_out, final_g):
    batch, seq, d_model = x.shape
    depth = w_ada.shape[0]
    w_a = conv_a_w.shape[-1]
    w_b = conv_b_w.shape[-1]
    assert depth == 1, "the final RMSNorm is fused into the single layer's output projection"
    assert batch <= SUBLANES
    c_pad = jnp.zeros((SUBLANES, d_model), _f32).at[:batch].set(c)
    mod = _ada(c_pad, w_ada[0], b_ada[0][None, :], tn=512)[:batch]
    mod3 = mod.reshape(batch, 3, d_model)
    gate = mod3[:, 2:3, :]
    w_in_bf = w_in[0].astype(_bf16)
    w_out_bf = w_out[0].astype(_bf16)
    h = _prenorm(x, mod3, norm_g[0][None, :], ts=512)
    y_a = _mixer_a(h, w_in_bf, conv_a_w[0], w_a, ts=1024, cw=256)
    y_b = _mixer_b(h, w_in_bf, conv_b_w[0], conv_b_b[0][None, :], ln_b_g[0][None, :],
                   ln_b_b[0][None, :], col0=4 * w_a, w_b=w_b, ts=512, cw=256)
    return _outproj(y_a, y_b, w_out_bf, x, gate, final_g[None, :], ts=512, tn=512)
```

```python
import functools

import jax
import jax.numpy as jnp
from jax import lax
from jax.experimental import pallas as pl
from jax.experimental.pallas import tpu as pltpu

EPS = 1e-6
CONV_A = 3
CONV_B = 31
SUBLANES = 8
HALO_A = 8
HALO_B = 32
CONV_ROWS = 128
VMEM_LIMIT = 60 * 1024 * 1024

_dot = functools.partial(jnp.dot, preferred_element_type=jnp.float32)
_bf16 = jnp.bfloat16
_f32 = jnp.float32


def _params(semantics):
    return pltpu.CompilerParams(dimension_semantics=semantics,
                                vmem_limit_bytes=VMEM_LIMIT)


def _ada_kernel(c_ref, w_ref, b_ref, o_ref):
    c_act = jax.nn.silu(c_ref[...]).astype(_bf16)
    o_ref[...] = _dot(c_act, w_ref[...].astype(_bf16)) + b_ref[...]


def _ada(c_pad, w_ada, b_ada, tn):
    m, d = c_pad.shape
    n = w_ada.shape[1]
    return pl.pallas_call(
        _ada_kernel,
        grid=(n // tn,),
        in_specs=[pl.BlockSpec((m, d), lambda j: (0, 0)),
                  pl.BlockSpec((d, tn), lambda j: (0, j)),
                  pl.BlockSpec((1, tn), lambda j: (0, j))],
        out_specs=pl.BlockSpec((m, tn), lambda j: (0, j)),
        out_shape=jax.ShapeDtypeStruct((m, n), _f32),
        compiler_params=_params(("arbitrary",)),
        name="ada",
    )(c_pad, w_ada, b_ada)


def _prenorm_kernel(x_ref, mod_ref, g_ref, h_ref):
    x = x_ref[0]
    ms = jnp.mean(x * x, axis=-1, keepdims=True)
    y = x * lax.rsqrt(ms + EPS) * g_ref[...]
    shift = mod_ref[0, 0:1, :]
    scale = mod_ref[0, 1:2, :]
    h_ref[0] = (y * (1.0 + scale) + shift).astype(_bf16)


def _prenorm(x, mod3, norm_g, ts):
    b, s, d = x.shape
    return pl.pallas_call(
        _prenorm_kernel,
        grid=(b, s // ts),
        in_specs=[pl.BlockSpec((1, ts, d), lambda i, t: (i, t, 0)),
                  pl.BlockSpec((1, 3, d), lambda i, t: (i, 0, 0)),
                  pl.BlockSpec((1, d), lambda i, t: (0, 0))],
        out_specs=pl.BlockSpec((1, ts, d), lambda i, t: (i, t, 0)),
        out_shape=jax.ShapeDtypeStruct((b, s, d), _bf16),
        compiler_params=_params(("arbitrary", "arbitrary")),
        name="prenorm",
    )(x, mod3, norm_g)


def _mixer_a_kernel(h_ref, wb_ref, wc_ref, wx_ref, wz_ref, cw_ref, y_ref,
                    carry_ref, ext_ref, *, ts):
    t = pl.program_id(1)
    j = pl.program_id(2)
    h = h_ref[0]

    @pl.when(t == 0)
    def _():
        carry_ref[j] = jnp.zeros(carry_ref.shape[1:], _f32)

    cx = _dot(h, wc_ref[...]) * _dot(h, wx_ref[...])
    ext_ref[0:HALO_A] = carry_ref[j]
    ext_ref[HALO_A:] = cx
    carry_ref[j] = cx[ts - HALO_A:]
    ext = ext_ref[...]
    w = cw_ref[...]
    conv = w[2:3] * cx
    for d in range(1, CONV_A):
        conv = conv + w[CONV_A - 1 - d:CONV_A - d] * pltpu.roll(ext, d, 0)[HALO_A:]
    y = _dot(h, wb_ref[...]) * conv * jax.nn.silu(_dot(h, wz_ref[...]))
    y_ref[0] = y.astype(_bf16)


def _mixer_a(h, w_in, conv_w, w_a, ts, cw):
    b, s, d = h.shape
    nj = w_a // cw

    def wspec(k):
        return pl.BlockSpec((d, cw), lambda i, t, j, k=k: (0, k * nj + j))

    return pl.pallas_call(
        functools.partial(_mixer_a_kernel, ts=ts),
        grid=(b, s // ts, nj),
        in_specs=[pl.BlockSpec((1, ts, d), lambda i, t, j: (i, t, 0)),
                  wspec(0), wspec(1), wspec(2), wspec(3),
                  pl.BlockSpec((CONV_A, cw), lambda i, t, j: (0, j))],
        out_specs=pl.BlockSpec((1, ts, cw), lambda i, t, j: (i, t, j)),
        out_shape=jax.ShapeDtypeStruct((b, s, w_a), _bf16),
        scratch_shapes=[pltpu.VMEM((nj, HALO_A, cw), _f32),
                        pltpu.VMEM((HALO_A + ts, cw), _f32)],
        compiler_params=_params(("arbitrary", "arbitrary", "arbitrary")),
        name="mixer_a",
    )(h, w_in, w_in, w_in, w_in, conv_w)


def _mixer_b_kernel(h_ref, wv_ref, wg_ref, wz_ref, cw_ref, cb_ref, lg_ref, lb_ref,
                    y_ref, ext_ref, carry_ref, u_ref, z_ref, k_ref, s1_ref, s2_ref,
                    mu_ref, rstd_ref, *, ts, cw, nj, n_tiles, tiles_per_seq):
    tt = pl.program_id(0)
    j = pl.program_id(1)
    jp = lax.rem(j + nj - 1, nj)
    width = nj * cw

    @pl.when((tt == 0) & (j == 0))
    def _():
        for ref in (ext_ref, carry_ref, u_ref, z_ref, k_ref, s1_ref, s2_ref, mu_ref, rstd_ref):
            ref[...] = jnp.zeros(ref.shape, _f32)

    def vpu_phase():
        first = jp == 0
        last = j == 0
        w = cw_ref[...]
        bias = cb_ref[...]
        for rb in range(ts // CONV_ROWS):
            rows = slice(rb * CONV_ROWS, (rb + 1) * CONV_ROWS)
            blk = ext_ref[rb * CONV_ROWS:(rb + 1) * CONV_ROWS + HALO_B, :]
            shifted = [blk] + [pltpu.roll(blk, r, 0) for r in range(1, SUBLANES)]
            acc = jnp.zeros((CONV_ROWS, cw), _f32) + bias
            for d in range(CONV_B):
                a, r = divmod(d, SUBLANES)
                lo = HALO_B - SUBLANES * a
                acc = acc + w[CONV_B - 1 - d:CONV_B - d] * shifted[r][lo:lo + CONV_ROWS]
            u_ref[jp, rows, :] = acc
            rs = jnp.sum(acc, axis=-1, keepdims=True)
            k = jnp.where(first, rs * (1.0 / cw), k_ref[rows])
            xc = acc - k
            s1 = jnp.where(first, 0.0, s1_ref[rows]) + (rs - cw * k)
            s2 = jnp.where(first, 0.0, s2_ref[rows]) + jnp.sum(xc * xc, axis=-1, keepdims=True)
            k_ref[rows] = k
            s1_ref[rows] = s1
            s2_ref[rows] = s2
            m1 = s1 * (1.0 / width)
            var = s2 * (1.0 / width) - m1 * m1
            mu_ref[rows] = jnp.where(last, k + m1, mu_ref[rows])
            rstd_ref[rows] = jnp.where(last, lax.rsqrt(var + EPS), rstd_ref[rows])
        ln = (u_ref[j] - mu_ref[...]) * rstd_ref[...] * lg_ref[...] + lb_ref[...]
        y_ref[0] = (jax.nn.silu(ln) * z_ref[j]).astype(_bf16)

    def mxu_phase():
        h = h_ref[0]
        glu = _dot(h, wv_ref[...]) * jax.nn.sigmoid(_dot(h, wg_ref[...]))
        seq_start = lax.rem(tt, tiles_per_seq) == 0
        ext_ref[0:HALO_B] = jnp.where(seq_start, 0.0, carry_ref[j])
        ext_ref[HALO_B:] = glu
        carry_ref[j] = glu[ts - HALO_B:]
        z_ref[j] = jax.nn.silu(_dot(h, wz_ref[...]))

    @pl.when(tt < n_tiles)
    def _():
        vpu_phase()
        mxu_phase()

    @pl.when(tt == n_tiles)
    def _():
        vpu_phase()


def _mixer_b(h, w_in, conv_w, conv_b, ln_g, ln_b, col0, w_b, ts, cw):
    b, s, d = h.shape
    nj = w_b // cw
    j0 = col0 // cw
    tiles_per_seq = s // ts
    n_tiles = b * tiles_per_seq

    def tile(tt):
        return tt // tiles_per_seq, lax.rem(tt, tiles_per_seq)

    def h_map(tt, j):
        return (*tile(jnp.minimum(tt, n_tiles - 1)), 0)

    def y_map(tt, j):
        return (*tile(jnp.maximum(tt - 1, 0)), j)

    def wspec(k):
        return pl.BlockSpec((d, cw), lambda tt, j, k=k: (0, j0 + k * nj + j))

    def pending(tt, j):
        return (0, lax.rem(j + nj - 1, nj))

    stat = pltpu.VMEM((ts, 1), _f32)
    return pl.pallas_call(
        functools.partial(_mixer_b_kernel, ts=ts, cw=cw, nj=nj, n_tiles=n_tiles,
                          tiles_per_seq=tiles_per_seq),
        grid=(n_tiles + 1, nj),
        in_specs=[pl.BlockSpec((1, ts, d), h_map),
                  wspec(0), wspec(1), wspec(2),
                  pl.BlockSpec((CONV_B, cw), pending),
                  pl.BlockSpec((1, cw), pending),
                  pl.BlockSpec((1, cw), lambda tt, j: (0, j)),
                  pl.BlockSpec((1, cw), lambda tt, j: (0, j))],
        out_specs=pl.BlockSpec((1, ts, cw), y_map),
        out_shape=jax.ShapeDtypeStruct((b, s, w_b), _bf16),
        scratch_shapes=[pltpu.VMEM((HALO_B + ts, cw), _f32),
                        pltpu.VMEM((nj, HALO_B, cw), _f32),
                        pltpu.VMEM((nj, ts, cw), _f32),
                        pltpu.VMEM((nj, ts, cw), _f32),
                        stat, stat, stat, stat, stat],
        compiler_params=_params(("arbitrary", "arbitrary")),
        name="mixer_b",
    )(h, w_in, w_in, w_in, conv_w, conv_b, ln_g, ln_b)


def _outproj_kernel(ya_ref, yb_ref, wa_ref, wb_ref, x_ref, gate_ref, fg_ref, o_ref,
                    r_ref, ss_ref, *, tn, nj):
    j = pl.program_id(2)

    @pl.when(j == 0)
    def _():
        ss_ref[...] = jnp.zeros(ss_ref.shape, _f32)

    delta = _dot(ya_ref[0], wa_ref[...]) + _dot(yb_ref[0], wb_ref[...])
    r = x_ref[0] + gate_ref[0] * delta
    r_ref[j] = r
    ss_ref[...] += jnp.sum(r * r, axis=-1, keepdims=True)

    @pl.when(j == nj - 1)
    def _():
        rstd = lax.rsqrt(ss_ref[...] * (1.0 / (nj * tn)) + EPS)
        for k in range(nj):
            cols = slice(k * tn, (k + 1) * tn)
            o_ref[0, :, cols] = r_ref[k] * rstd * fg_ref[:, cols]


def _outproj(y_a, y_b, w_out, x, gate, final_g, ts, tn):
    b, s, d = x.shape
    w_a = y_a.shape[-1]
    w_b = y_b.shape[-1]
    nj = d // tn
    assert w_a % w_b == 0
    return pl.pallas_call(
        functools.partial(_outproj_kernel, tn=tn, nj=nj),
        grid=(b, s // ts, nj),
        in_specs=[pl.BlockSpec((1, ts, w_a), lambda i, t, j: (i, t, 0)),
                  pl.BlockSpec((1, ts, w_b), lambda i, t, j: (i, t, 0)),
                  pl.BlockSpec((w_a, tn), lambda i, t, j: (0, j)),
                  pl.BlockSpec((w_b, tn), lambda i, t, j: (w_a // w_b, j)),
                  pl.BlockSpec((1, ts, tn), lambda i, t, j: (i, t, j)),
                  pl.BlockSpec((1, 1, tn), lambda i, t, j: (i, 0, j)),
                  pl.BlockSpec((1, d), lambda i, t, j: (0, 0))],
        out_specs=pl.BlockSpec((1, ts, d), lambda i, t, j: (i, t, 0)),
        out_shape=jax.ShapeDtypeStruct((b, s, d), _f32),
        scratch_shapes=[pltpu.VMEM((nj, ts, tn), _f32),
                        pltpu.VMEM((ts, 1), _f32)],
        compiler_params=_params(("arbitrary", "arbitrary", "arbitrary")),
        name="outproj",
    )(y_a, y_b, w_out, w_out, x, gate, final_g)


def kernel(x, c, norm_g, w_ada, b_ada, w_in, conv_a_w, conv_b_w, conv_b_b, ln_b_g,
           ln_b_b, w_out, final_g):
    batch, seq, d_model = x.shape
    depth = w_ada.shape[0]
    w_a = conv_a_w.shape[-1]
    w_b = conv_b_w.shape[-1]
    assert depth == 1, "the final RMSNorm is fused into the single layer's output projection"
    assert batch <= SUBLANES
    c_pad = jnp.zeros((SUBLANES, d_model), _f32).at[:batch].set(c)
    mod = _ada(c_pad, w_ada[0], b_ada[0][None, :], tn=512)[:batch]
    mod3 = mod.reshape(batch, 3, d_model)
    gate = mod3[:, 2:3, :]
    w_in_bf = w_in[0].astype(_bf16)
    w_out_bf = w_out[0].astype(_bf16)
    h = _prenorm(x, mod3, norm_g[0][None, :], ts=512)
    y_a = _mixer_a(h, w_in_bf, conv_a_w[0], w_a, ts=1024, cw=256)
    y_b = _mixer_b(h, w_in_bf, conv_b_w[0], conv_b_b[0][None, :], ln_b_g[0][None, :],
                   ln_b_b[0][None, :], col0=4 * w_a, w_b=w_b, ts=1024, cw=256)
    return _outproj(y_a, y_b, w_out_bf, x, gate, final_g[None, :], ts=512, tn=512)
```

```python
import functools

import jax
import jax.numpy as jnp
from jax import lax
from jax.experimental import pallas as pl
from jax.experimental.pallas import tpu as pltpu

EPS = 1e-6
CONV_A = 3
CONV_B = 31
SUBLANES = 8
HALO_A = 8
HALO_B = 32
LANES = 128
SHIFT_ROWS = 256
ACC_ROWS = 32
VMEM_LIMIT = 60 * 1024 * 1024

_dot = functools.partial(jnp.dot, preferred_element_type=jnp.float32)
_bf16 = jnp.bfloat16
_f32 = jnp.float32


def _params(semantics):
    return pltpu.CompilerParams(dimension_semantics=semantics,
                                vmem_limit_bytes=VMEM_LIMIT)


def _ada_kernel(c_ref, w_ref, b_ref, o_ref):
    c_act = jax.nn.silu(c_ref[...]).astype(_bf16)
    o_ref[...] = _dot(c_act, w_ref[...].astype(_bf16)) + b_ref[...]


def _ada(c_pad, w_ada, b_ada, tn):
    m, d = c_pad.shape
    n = w_ada.shape[1]
    return pl.pallas_call(
        _ada_kernel,
        grid=(n // tn,),
        in_specs=[pl.BlockSpec((m, d), lambda j: (0, 0)),
                  pl.BlockSpec((d, tn), lambda j: (0, j)),
                  pl.BlockSpec((1, tn), lambda j: (0, j))],
        out_specs=pl.BlockSpec((m, tn), lambda j: (0, j)),
        out_shape=jax.ShapeDtypeStruct((m, n), _f32),
        compiler_params=_params(("arbitrary",)),
        name="ada",
    )(c_pad, w_ada, b_ada)


def _prenorm_kernel(x_ref, mod_ref, g_ref, h_ref):
    x = x_ref[0]
    ms = jnp.mean(x * x, axis=-1, keepdims=True)
    y = x * lax.rsqrt(ms + EPS) * g_ref[...]
    shift = mod_ref[0, 0:1, :]
    scale = mod_ref[0, 1:2, :]
    h_ref[0] = (y * (1.0 + scale) + shift).astype(_bf16)


def _prenorm(x, mod3, norm_g, ts):
    b, s, d = x.shape
    return pl.pallas_call(
        _prenorm_kernel,
        grid=(b, s // ts),
        in_specs=[pl.BlockSpec((1, ts, d), lambda i, t: (i, t, 0)),
                  pl.BlockSpec((1, 3, d), lambda i, t: (i, 0, 0)),
                  pl.BlockSpec((1, d), lambda i, t: (0, 0))],
        out_specs=pl.BlockSpec((1, ts, d), lambda i, t: (i, t, 0)),
        out_shape=jax.ShapeDtypeStruct((b, s, d), _bf16),
        compiler_params=_params(("arbitrary", "arbitrary")),
        name="prenorm",
    )(x, mod3, norm_g)


def _mixer_a_kernel(h_ref, wb_ref, wc_ref, wx_ref, wz_ref, cw_ref, y_ref,
                    carry_ref, ext_ref, *, ts):
    t = pl.program_id(1)
    j = pl.program_id(2)
    h = h_ref[0]

    @pl.when(t == 0)
    def _():
        carry_ref[j] = jnp.zeros(carry_ref.shape[1:], _f32)

    cx = _dot(h, wc_ref[...]) * _dot(h, wx_ref[...])
    ext_ref[0:HALO_A] = carry_ref[j]
    ext_ref[HALO_A:] = cx
    carry_ref[j] = cx[ts - HALO_A:]
    ext = ext_ref[...]
    w = cw_ref[...]
    conv = w[2:3] * cx
    for d in range(1, CONV_A):
        conv = conv + w[CONV_A - 1 - d:CONV_A - d] * pltpu.roll(ext, d, 0)[HALO_A:]
    y = _dot(h, wb_ref[...]) * conv * jax.nn.silu(_dot(h, wz_ref[...]))
    y_ref[0] = y.astype(_bf16)


def _mixer_a(h, w_in, conv_w, w_a, ts, cw):
    b, s, d = h.shape
    nj = w_a // cw

    def wspec(k):
        return pl.BlockSpec((d, cw), lambda i, t, j, k=k: (0, k * nj + j))

    return pl.pallas_call(
        functools.partial(_mixer_a_kernel, ts=ts),
        grid=(b, s // ts, nj),
        in_specs=[pl.BlockSpec((1, ts, d), lambda i, t, j: (i, t, 0)),
                  wspec(0), wspec(1), wspec(2), wspec(3),
                  pl.BlockSpec((CONV_A, cw), lambda i, t, j: (0, j))],
        out_specs=pl.BlockSpec((1, ts, cw), lambda i, t, j: (i, t, j)),
        out_shape=jax.ShapeDtypeStruct((b, s, w_a), _bf16),
        scratch_shapes=[pltpu.VMEM((nj, HALO_A, cw), _f32),
                        pltpu.VMEM((HALO_A + ts, cw), _f32)],
        compiler_params=_params(("arbitrary", "arbitrary", "arbitrary")),
        name="mixer_a",
    )(h, w_in, w_in, w_in, w_in, conv_w)


def _mixer_b_kernel(h_ref, wv_ref, wg_ref, wz_ref, cw_ref, cb_ref, lg_ref, lb_ref,
                    y_ref, ext_ref, sh_ref, carry_ref, u_ref, z_ref, k_ref, s1_ref, s2_ref,
                    mu_ref, rstd_ref, *, ts, cw, nj, n_tiles, tiles_per_seq):
    tt = pl.program_id(0)
    j = pl.program_id(1)
    jp = lax.rem(j + nj - 1, nj)
    width = nj * cw

    @pl.when((tt == 0) & (j == 0))
    def _():
        for ref in (ext_ref, carry_ref, u_ref, z_ref, k_ref, s1_ref, s2_ref, mu_ref, rstd_ref):
            ref[...] = jnp.zeros(ref.shape, ref.dtype)

    def conv_block(b):
        first = jp == 0
        last = j == 0
        nv = (SHIFT_ROWS + HALO_B) // SUBLANES
        x3 = ext_ref[b * SHIFT_ROWS:(b + 1) * SHIFT_ROWS + HALO_B, :].reshape(nv, SUBLANES, cw)
        sh_ref[0] = x3
        sub = lax.broadcasted_iota(jnp.int32, x3.shape, 1)
        for r in range(1, SUBLANES):
            rot = pltpu.roll(x3, r, 1)
            prev = jnp.concatenate([rot[:1], rot[:-1]], axis=0)
            sh_ref[r] = jnp.where(sub < r, prev, rot)
        nvb = ACC_ROWS // SUBLANES
        for rb in range(SHIFT_ROWS // ACC_ROWS):
            accs = []
            for lc in range(cw // LANES):
                lanes = slice(lc * LANES, (lc + 1) * LANES)
                acc = jnp.zeros((nvb, SUBLANES, LANES), _f32) + cb_ref[:, lanes]
                for d in range(CONV_B):
                    a, r = divmod(d, SUBLANES)
                    v0 = HALO_B // SUBLANES - a + rb * nvb
                    acc = acc + cw_ref[CONV_B - 1 - d:CONV_B - d, lanes] * sh_ref[r, v0:v0 + nvb, :, lanes]
                accs.append(acc)
            row0 = b * SHIFT_ROWS + rb * ACC_ROWS
            rows = slice(row0, row0 + ACC_ROWS)
            u_ref[jp, row0 // SUBLANES:row0 // SUBLANES + nvb] = jnp.concatenate(accs, axis=-1)
            rs = jnp.sum(sum(accs), axis=-1, keepdims=True).reshape(ACC_ROWS, 1)
            k = jnp.where(first, rs * (1.0 / cw), k_ref[rows])
            k3 = k.reshape(nvb, SUBLANES, 1)
            sq = sum((acc - k3) * (acc - k3) for acc in accs)
            s1 = jnp.where(first, 0.0, s1_ref[rows]) + (rs - cw * k)
            s2 = (jnp.where(first, 0.0, s2_ref[rows])
                  + jnp.sum(sq, axis=-1, keepdims=True).reshape(ACC_ROWS, 1))
            k_ref[rows] = k
            s1_ref[rows] = s1
            s2_ref[rows] = s2
            m1 = s1 * (1.0 / width)
            var = s2 * (1.0 / width) - m1 * m1
            mu_ref[rows] = jnp.where(last, k + m1, mu_ref[rows])
            rstd_ref[rows] = jnp.where(last, lax.rsqrt(var + EPS), rstd_ref[rows])

    def vpu_phase():
        for b in range(ts // SHIFT_ROWS):
            conv_block(b)
        u = u_ref[j].reshape(ts, cw)
        ln = (u - mu_ref[...]) * rstd_ref[...] * lg_ref[...] + lb_ref[...]
        y_ref[0] = (jax.nn.silu(ln) * z_ref[j]).astype(_bf16)

    def mxu_phase():
        h = h_ref[0]
        glu = _dot(h, wv_ref[...]) * jax.nn.sigmoid(_dot(h, wg_ref[...]))
        seq_start = lax.rem(tt, tiles_per_seq) == 0
        ext_ref[0:HALO_B] = jnp.where(seq_start, 0.0, carry_ref[j])
        ext_ref[HALO_B:] = glu
        carry_ref[j] = glu[ts - HALO_B:]
        z_ref[j] = jax.nn.silu(_dot(h, wz_ref[...])).astype(_bf16)

    @pl.when(tt < n_tiles)
    def _():
        vpu_phase()
        mxu_phase()

    @pl.when(tt == n_tiles)
    def _():
        vpu_phase()


def _mixer_b(h, w_in, conv_w, conv_b, ln_g, ln_b, col0, w_b, ts, cw):
    b, s, d = h.shape
    nj = w_b // cw
    j0 = col0 // cw
    tiles_per_seq = s // ts
    n_tiles = b * tiles_per_seq

    def tile(tt):
        return tt // tiles_per_seq, lax.rem(tt, tiles_per_seq)

    def h_map(tt, j):
        return (*tile(jnp.minimum(tt, n_tiles - 1)), 0)

    def y_map(tt, j):
        return (*tile(jnp.maximum(tt - 1, 0)), j * jnp.minimum(tt, 1))

    def wspec(k):
        return pl.BlockSpec((d, cw), lambda tt, j, k=k: (0, j0 + k * nj + j))

    def pending(tt, j):
        return (0, lax.rem(j + nj - 1, nj))

    stat = pltpu.VMEM((ts, 1), _f32)
    return pl.pallas_call(
        functools.partial(_mixer_b_kernel, ts=ts, cw=cw, nj=nj, n_tiles=n_tiles,
                          tiles_per_seq=tiles_per_seq),
        grid=(n_tiles + 1, nj),
        in_specs=[pl.BlockSpec((1, ts, d), h_map),
                  wspec(0), wspec(1), wspec(2),
                  pl.BlockSpec((CONV_B, cw), pending),
                  pl.BlockSpec((1, cw), pending),
                  pl.BlockSpec((1, cw), lambda tt, j: (0, j)),
                  pl.BlockSpec((1, cw), lambda tt, j: (0, j))],
        out_specs=pl.BlockSpec((1, ts, cw), y_map),
        out_shape=jax.ShapeDtypeStruct((b, s, w_b), _bf16),
        scratch_shapes=[pltpu.VMEM((HALO_B + ts, cw), _f32),
                        pltpu.VMEM((SUBLANES, (SHIFT_ROWS + HALO_B) // SUBLANES, SUBLANES, cw), _f32),
                        pltpu.VMEM((nj, HALO_B, cw), _f32),
                        pltpu.VMEM((nj, ts // SUBLANES, SUBLANES, cw), _f32),
                        pltpu.VMEM((nj, ts, cw), _bf16),
                        stat, stat, stat, stat, stat],
        compiler_params=_params(("arbitrary", "arbitrary")),
        name="mixer_b",
    )(h, w_in, w_in, w_in, conv_w, conv_b, ln_g, ln_b)


def _outproj_kernel(ya_ref, yb_ref, wa_ref, wb_ref, x_ref, gate_ref, fg_ref, o_ref,
                    r_ref, ss_ref, *, tn, nj):
    j = pl.program_id(2)

    @pl.when(j == 0)
    def _():
        ss_ref[...] = jnp.zeros(ss_ref.shape, _f32)

    delta = _dot(ya_ref[0], wa_ref[...]) + _dot(yb_ref[0], wb_ref[...])
    r = x_ref[0] + gate_ref[0] * delta
    r_ref[j] = r
    ss_ref[...] += jnp.sum(r * r, axis=-1, keepdims=True)

    @pl.when(j == nj - 1)
    def _():
        rstd = lax.rsqrt(ss_ref[...] * (1.0 / (nj * tn)) + EPS)
        for k in range(nj):
            cols = slice(k * tn, (k + 1) * tn)
            o_ref[0, :, cols] = r_ref[k] * rstd * fg_ref[:, cols]


def _outproj(y_a, y_b, w_out, x, gate, final_g, ts, tn):
    b, s, d = x.shape
    w_a = y_a.shape[-1]
    w_b = y_b.shape[-1]
    nj = d // tn
    assert w_a % w_b == 0
    return pl.pallas_call(
        functools.partial(_outproj_kernel, tn=tn, nj=nj),
        grid=(b, s // ts, nj),
        in_specs=[pl.BlockSpec((1, ts, w_a), lambda i, t, j: (i, t, 0)),
                  pl.BlockSpec((1, ts, w_b), lambda i, t, j: (i, t, 0)),
                  pl.BlockSpec((w_a, tn), lambda i, t, j: (0, j)),
                  pl.BlockSpec((w_b, tn), lambda i, t, j: (w_a // w_b, j)),
                  pl.BlockSpec((1, ts, tn), lambda i, t, j: (i, t, j)),
                  pl.BlockSpec((1, 1, tn), lambda i, t, j: (i, 0, j)),
                  pl.BlockSpec((1, d), lambda i, t, j: (0, 0))],
        out_specs=pl.BlockSpec((1, ts, d), lambda i, t, j: (i, t, 0)),
        out_shape=jax.ShapeDtypeStruct((b, s, d), _f32),
        scratch_shapes=[pltpu.VMEM((nj, ts, tn), _f32),
                        pltpu.VMEM((ts, 1), _f32)],
        compiler_params=_params(("arbitrary", "arbitrary", "arbitrary")),
        name="outproj",
    )(y_a, y_b, w_out, w_out, x, gate, final_g)


def kernel(x, c, norm_g, w_ada, b_ada, w_in, conv_a_w, conv_b_w, conv_b_b, ln_b_g,
           ln_b_b, w_out, final_g):
    batch, seq, d_model = x.shape
    depth = w_ada.shape[0]
    w_a = conv_a_w.shape[-1]
    w_b = conv_b_w.shape[-1]
    assert depth == 1, "the final RMSNorm is fused into the single layer's output projection"
    assert batch <= SUBLANES
    c_pad = jnp.zeros((SUBLANES, d_model), _f32).at[:batch].set(c)
    mod = _ada(c_pad, w_ada[0], b_ada[0][None, :], tn=512)[:batch]
    mod3 = mod.reshape(batch, 3, d_model)
    gate = mod3[:, 2:3, :]
    w_in_bf = w_in[0].astype(_bf16)
    w_out_bf = w_out[0].astype(_bf16)
    h = _prenorm(x, mod3, norm_g[0][None, :], ts=512)
    y_a = _mixer_a(h, w_in_bf, conv_a_w[0], w_a, ts=1024, cw=256)
    y_b = _mixer_b(h, w_in_bf, conv_b_w[0], conv_b_b[0][None, :], ln_b_g[0][None, :],
                   ln_b_b[0][None, :], col0=4 * w_a, w_b=w_b, ts=1024, cw=256)
    return _outproj(y_a, y_b, w_out_bf, x, gate, final_g[None, :], ts=512, tn=512)
```

```python
import functools

import jax
import jax.numpy as jnp
from jax import lax
from jax.experimental import pallas as pl
from jax.experimental.pallas import tpu as pltpu

EPS = 1e-6
CONV_A = 3
CONV_B = 31
SUBLANES = 8
HALO_A = 8
HALO_B = 32
LANES = 128
MXU_COLS = 256
SHIFT_ROWS = 256
ACC_ROWS = 32
VMEM_LIMIT = 60 * 1024 * 1024

_dot = functools.partial(jnp.dot, preferred_element_type=jnp.float32)
_bf16 = jnp.bfloat16
_f32 = jnp.float32


def _params(semantics):
    return pltpu.CompilerParams(dimension_semantics=semantics,
                                vmem_limit_bytes=VMEM_LIMIT)


def _ada_kernel(c_ref, w_ref, b_ref, o_ref):
    c_act = jax.nn.silu(c_ref[...]).astype(_bf16)
    o_ref[...] = _dot(c_act, w_ref[...].astype(_bf16)) + b_ref[...]


def _ada(c_pad, w_ada, b_ada, tn):
    m, d = c_pad.shape
    n = w_ada.shape[1]
    return pl.pallas_call(
        _ada_kernel,
        grid=(n // tn,),
        in_specs=[pl.BlockSpec((m, d), lambda j: (0, 0)),
                  pl.BlockSpec((d, tn), lambda j: (0, j)),
                  pl.BlockSpec((1, tn), lambda j: (0, j))],
        out_specs=pl.BlockSpec((m, tn), lambda j: (0, j)),
        out_shape=jax.ShapeDtypeStruct((m, n), _f32),
        compiler_params=_params(("arbitrary",)),
        name="ada",
    )(c_pad, w_ada, b_ada)


def _prenorm_kernel(x_ref, mod_ref, g_ref, h_ref):
    x = x_ref[0]
    ms = jnp.mean(x * x, axis=-1, keepdims=True)
    y = x * lax.rsqrt(ms + EPS) * g_ref[...]
    shift = mod_ref[0, 0:1, :]
    scale = mod_ref[0, 1:2, :]
    h_ref[0] = (y * (1.0 + scale) + shift).astype(_bf16)


def _prenorm(x, mod3, norm_g, ts):
    b, s, d = x.shape
    return pl.pallas_call(
        _prenorm_kernel,
        grid=(b, s // ts),
        in_specs=[pl.BlockSpec((1, ts, d), lambda i, t: (i, t, 0)),
                  pl.BlockSpec((1, 3, d), lambda i, t: (i, 0, 0)),
                  pl.BlockSpec((1, d), lambda i, t: (0, 0))],
        out_specs=pl.BlockSpec((1, ts, d), lambda i, t: (i, t, 0)),
        out_shape=jax.ShapeDtypeStruct((b, s, d), _bf16),
        compiler_params=_params(("arbitrary", "arbitrary")),
        name="prenorm",
    )(x, mod3, norm_g)


def _mixer_a_kernel(h_ref, wb_ref, wc_ref, wx_ref, wz_ref, cw_ref, y_ref,
                    carry_ref, ext_ref, *, ts):
    t = pl.program_id(1)
    j = pl.program_id(2)
    h = h_ref[0]

    @pl.when(t == 0)
    def _():
        carry_ref[j] = jnp.zeros(carry_ref.shape[1:], _f32)

    for cols in (slice(c0, c0 + MXU_COLS) for c0 in range(0, y_ref.shape[-1], MXU_COLS)):
        cx = _dot(h, wc_ref[:, cols]) * _dot(h, wx_ref[:, cols])
        ext_ref[0:HALO_A] = carry_ref[j, :, cols]
        ext_ref[HALO_A:] = cx
        carry_ref[j, :, cols] = cx[ts - HALO_A:]
        ext = ext_ref[...]
        w = cw_ref[:, cols]
        conv = w[2:3] * cx
        for d in range(1, CONV_A):
            conv = conv + w[CONV_A - 1 - d:CONV_A - d] * pltpu.roll(ext, d, 0)[HALO_A:]
        y = _dot(h, wb_ref[:, cols]) * conv * jax.nn.silu(_dot(h, wz_ref[:, cols]))
        y_ref[0, :, cols] = y.astype(_bf16)


def _mixer_a(h, w_in, conv_w, w_a, ts, cw):
    b, s, d = h.shape
    nj = w_a // cw

    def wspec(k):
        return pl.BlockSpec((d, cw), lambda i, t, j, k=k: (0, k * nj + j))

    return pl.pallas_call(
        functools.partial(_mixer_a_kernel, ts=ts),
        grid=(b, s // ts, nj),
        in_specs=[pl.BlockSpec((1, ts, d), lambda i, t, j: (i, t, 0)),
                  wspec(0), wspec(1), wspec(2), wspec(3),
                  pl.BlockSpec((CONV_A, cw), lambda i, t, j: (0, j))],
        out_specs=pl.BlockSpec((1, ts, cw), lambda i, t, j: (i, t, j)),
        out_shape=jax.ShapeDtypeStruct((b, s, w_a), _bf16),
        scratch_shapes=[pltpu.VMEM((nj, HALO_A, cw), _f32),
                        pltpu.VMEM((HALO_A + ts, MXU_COLS), _f32)],
        compiler_params=_params(("arbitrary", "arbitrary", "arbitrary")),
        name="mixer_a",
    )(h, w_in, w_in, w_in, w_in, conv_w)


def _mixer_b_kernel(h_ref, wv_ref, wg_ref, wz_ref, cw_ref, cb_ref, lg_ref, lb_ref,
                    y_ref, ext_ref, sh_ref, carry_ref, u_ref, z_ref, k_ref, s1_ref, s2_ref,
                    mu_ref, rstd_ref, *, ts, cw, nj, n_tiles, tiles_per_seq):
    tt = pl.program_id(0)
    j = pl.program_id(1)
    jp = lax.rem(j + nj - 1, nj)
    width = nj * cw

    @pl.when((tt == 0) & (j == 0))
    def _():
        for ref in (ext_ref, carry_ref, u_ref, z_ref, k_ref, s1_ref, s2_ref, mu_ref, rstd_ref):
            ref[...] = jnp.zeros(ref.shape, ref.dtype)

    def conv_block(b):
        first = jp == 0
        last = j == 0
        nv = (SHIFT_ROWS + HALO_B) // SUBLANES
        x3 = ext_ref[b * SHIFT_ROWS:(b + 1) * SHIFT_ROWS + HALO_B, :].reshape(nv, SUBLANES, cw)
        sh_ref[0] = x3
        sub = lax.broadcasted_iota(jnp.int32, x3.shape, 1)
        for r in range(1, SUBLANES):
            rot = pltpu.roll(x3, r, 1)
            prev = jnp.concatenate([rot[:1], rot[:-1]], axis=0)
            sh_ref[r] = jnp.where(sub < r, prev, rot)
        nvb = ACC_ROWS // SUBLANES
        for rb in range(SHIFT_ROWS // ACC_ROWS):
            accs = []
            for lc in range(cw // LANES):
                lanes = slice(lc * LANES, (lc + 1) * LANES)
                acc = jnp.zeros((nvb, SUBLANES, LANES), _f32) + cb_ref[:, lanes]
                for d in range(CONV_B):
                    a, r = divmod(d, SUBLANES)
                    v0 = HALO_B // SUBLANES - a + rb * nvb
                    acc = acc + cw_ref[CONV_B - 1 - d:CONV_B - d, lanes] * sh_ref[r, v0:v0 + nvb, :, lanes]
                accs.append(acc)
            row0 = b * SHIFT_ROWS + rb * ACC_ROWS
            rows = slice(row0, row0 + ACC_ROWS)
            u_ref[jp, row0 // SUBLANES:row0 // SUBLANES + nvb] = jnp.concatenate(accs, axis=-1)
            rs = jnp.sum(sum(accs), axis=-1, keepdims=True).reshape(ACC_ROWS, 1)
            k = jnp.where(first, rs * (1.0 / cw), k_ref[rows])
            k3 = k.reshape(nvb, SUBLANES, 1)
            sq = sum((acc - k3) * (acc - k3) for acc in accs)
            s1 = jnp.where(first, 0.0, s1_ref[rows]) + (rs - cw * k)
            s2 = (jnp.where(first, 0.0, s2_ref[rows])
                  + jnp.sum(sq, axis=-1, keepdims=True).reshape(ACC_ROWS, 1))
            k_ref[rows] = k
            s1_ref[rows] = s1
            s2_ref[rows] = s2
            m1 = s1 * (1.0 / width)
            var = s2 * (1.0 / width) - m1 * m1
            mu_ref[rows] = jnp.where(last, k + m1, mu_ref[rows])
            rstd_ref[rows] = jnp.where(last, lax.rsqrt(var + EPS), rstd_ref[rows])

    def vpu_phase():
        for b in range(ts // SHIFT_ROWS):
            conv_block(b)
        u = u_ref[j].reshape(ts, cw)
        ln = (u - mu_ref[...]) * rstd_ref[...] * lg_ref[...] + lb_ref[...]
        y_ref[0] = (jax.nn.silu(ln) * z_ref[j]).astype(_bf16)

    def mxu_phase():
        h = h_ref[0]
        glu = _dot(h, wv_ref[...]) * jax.nn.sigmoid(_dot(h, wg_ref[...]))
        seq_start = lax.rem(tt, tiles_per_seq) == 0
        ext_ref[0:HALO_B] = jnp.where(seq_start, 0.0, carry_ref[j])
        ext_ref[HALO_B:] = glu
        carry_ref[j] = glu[ts - HALO_B:]
        z_ref[j] = jax.nn.silu(_dot(h, wz_ref[...])).astype(_bf16)

    @pl.when(tt < n_tiles)
    def _():
        vpu_phase()
        mxu_phase()

    @pl.when(tt == n_tiles)
    def _():
        vpu_phase()


def _mixer_b(h, w_in, conv_w, conv_b, ln_g, ln_b, col0, w_b, ts, cw):
    b, s, d = h.shape
    nj = w_b // cw
    j0 = col0 // cw
    tiles_per_seq = s // ts
    n_tiles = b * tiles_per_seq

    def tile(tt):
        return tt // tiles_per_seq, lax.rem(tt, tiles_per_seq)

    def h_map(tt, j):
        return (*tile(jnp.minimum(tt, n_tiles - 1)), 0)

    def y_map(tt, j):
        return (*tile(jnp.maximum(tt - 1, 0)), j * jnp.minimum(tt, 1))

    def wspec(k):
        return pl.BlockSpec((d, cw), lambda tt, j, k=k: (0, j0 + k * nj + j))

    def pending(tt, j):
        return (0, lax.rem(j + nj - 1, nj))

    stat = pltpu.VMEM((ts, 1), _f32)
    return pl.pallas_call(
        functools.partial(_mixer_b_kernel, ts=ts, cw=cw, nj=nj, n_tiles=n_tiles,
                          tiles_per_seq=tiles_per_seq),
        grid=(n_tiles + 1, nj),
        in_specs=[pl.BlockSpec((1, ts, d), h_map),
                  wspec(0), wspec(1), wspec(2),
                  pl.BlockSpec((CONV_B, cw), pending),
                  pl.BlockSpec((1, cw), pending),
                  pl.BlockSpec((1, cw), lambda tt, j: (0, j)),
                  pl.BlockSpec((1, cw), lambda tt, j: (0, j))],
        out_specs=pl.BlockSpec((1, ts, cw), y_map),
        out_shape=jax.ShapeDtypeStruct((b, s, w_b), _bf16),
        scratch_shapes=[pltpu.VMEM((HALO_B + ts, cw), _f32),
                        pltpu.VMEM((SUBLANES, (SHIFT_ROWS + HALO_B) // SUBLANES, SUBLANES, cw), _f32),
                        pltpu.VMEM((nj, HALO_B, cw), _f32),
                        pltpu.VMEM((nj, ts // SUBLANES, SUBLANES, cw), _f32),
                        pltpu.VMEM((nj, ts, cw), _bf16),
                        stat, stat, stat, stat, stat],
        compiler_params=_params(("arbitrary", "arbitrary")),
        name="mixer_b",
    )(h, w_in, w_in, w_in, conv_w, conv_b, ln_g, ln_b)


def _outproj_kernel(ya_ref, yb_ref, w_ref, x_ref, gate_ref, fg_ref, o_ref, *, tn):
    ts, d = o_ref.shape[1:]
    w_a = ya_ref.shape[-1]
    ya = ya_ref[0]
    yb = yb_ref[0]
    ss = jnp.zeros((ts, 1), _f32)
    for k in range(d // tn):
        cols = slice(k * tn, (k + 1) * tn)
        delta = _dot(ya, w_ref[0:w_a, cols]) + _dot(yb, w_ref[w_a:, cols])
        r = x_ref[0, :, cols] + gate_ref[0, :, cols] * delta
        o_ref[0, :, cols] = r
        ss = ss + jnp.sum(r * r, axis=-1, keepdims=True)
    rstd = lax.rsqrt(ss * (1.0 / d) + EPS)
    o_ref[0] = o_ref[0] * rstd * fg_ref[...]


def _outproj(y_a, y_b, w_out, x, gate, final_g, ts, tn):
    b, s, d = x.shape
    w_a = y_a.shape[-1]
    w_b = y_b.shape[-1]
    return pl.pallas_call(
        functools.partial(_outproj_kernel, tn=tn),
        grid=(b, s // ts),
        in_specs=[pl.BlockSpec((1, ts, w_a), lambda i, t: (i, t, 0)),
                  pl.BlockSpec((1, ts, w_b), lambda i, t: (i, t, 0)),
                  pl.BlockSpec((w_a + w_b, d), lambda i, t: (0, 0), pipeline_mode=pl.Buffered(1)),
                  pl.BlockSpec((1, ts, d), lambda i, t: (i, t, 0)),
                  pl.BlockSpec((1, 1, d), lambda i, t: (i, 0, 0)),
                  pl.BlockSpec((1, d), lambda i, t: (0, 0))],
        out_specs=pl.BlockSpec((1, ts, d), lambda i, t: (i, t, 0)),
        out_shape=jax.ShapeDtypeStruct((b, s, d), _f32),
        compiler_params=_params(("arbitrary", "arbitrary")),
        name="outproj",
    )(y_a, y_b, w_out, x, gate, final_g)


def kernel(x, c, norm_g, w_ada, b_ada, w_in, conv_a_w, conv_b_w, conv_b_b, ln_b_g,
           ln_b_b, w_out, final_g):
    batch, seq, d_model = x.shape
    depth = w_ada.shape[0]
    w_a = conv_a_w.shape[-1]
    w_b = conv_b_w.shape[-1]
    assert depth == 1, "the final RMSNorm is fused into the single layer's output projection"
    assert batch <= SUBLANES
    c_pad = jnp.zeros((SUBLANES, d_model), _f32).at[:batch].set(c)
    mod = _ada(c_pad, w_ada[0], b_ada[0][None, :], tn=512)[:batch]
    mod3 = mod.reshape(batch, 3, d_model)
    gate = mod3[:, 2:3, :]
    w_in_bf = w_in[0].astype(_bf16)
    w_out_bf = w_out[0].astype(_bf16)
    h = _prenorm(x, mod3, norm_g[0][None, :], ts=512)
    y_a = _mixer_a(h, w_in_bf, conv_a_w[0], w_a, ts=1024, cw=512)
    y_b = _mixer_b(h, w_in_bf, conv_b_w[0], conv_b_b[0][None, :], ln_b_g[0][None, :],
                   ln_b_b[0][None, :], col0=4 * w_a, w_b=w_b, ts=1024, cw=256)
    return _outproj(y_a, y_b, w_out_bf, x, gate, final_g[None, :], ts=256, tn=512)
```

```python
import functools

import jax
import jax.numpy as jnp
from jax import lax
from jax.experimental import pallas as pl
from jax.experimental.pallas import tpu as pltpu

EPS = 1e-6
CONV_A = 3
CONV_B = 31
SUBLANES = 8
HALO_A = 8
HALO_B = 32
LANES = 128
MXU_COLS = 256
SHIFT_ROWS = 256
ACC_ROWS = 32
VMEM_LIMIT = 60 * 1024 * 1024

_dot = functools.partial(jnp.dot, preferred_element_type=jnp.float32)
_bf16 = jnp.bfloat16
_f32 = jnp.float32


def _params(semantics):
    return pltpu.CompilerParams(dimension_semantics=semantics,
                                vmem_limit_bytes=VMEM_LIMIT)


def _ada_kernel(c_ref, w_ref, b_ref, o_ref):
    c_act = jax.nn.silu(c_ref[...]).astype(_bf16)
    o_ref[...] = _dot(c_act, w_ref[...].astype(_bf16)) + b_ref[...]


def _ada(c_pad, w_ada, b_ada, tn):
    m, d = c_pad.shape
    n = w_ada.shape[1]
    return pl.pallas_call(
        _ada_kernel,
        grid=(n // tn,),
        in_specs=[pl.BlockSpec((m, d), lambda j: (0, 0)),
                  pl.BlockSpec((d, tn), lambda j: (0, j)),
                  pl.BlockSpec((1, tn), lambda j: (0, j))],
        out_specs=pl.BlockSpec((m, tn), lambda j: (0, j)),
        out_shape=jax.ShapeDtypeStruct((m, n), _f32),
        compiler_params=_params(("arbitrary",)),
        name="ada",
    )(c_pad, w_ada, b_ada)


def _prenorm_kernel(x_ref, mod_ref, g_ref, h_ref):
    x = x_ref[0]
    ms = jnp.mean(x * x, axis=-1, keepdims=True)
    y = x * lax.rsqrt(ms + EPS) * g_ref[...]
    shift = mod_ref[0, 0:1, :]
    scale = mod_ref[0, 1:2, :]
    h_ref[0] = (y * (1.0 + scale) + shift).astype(_bf16)


def _prenorm(x, mod3, norm_g, ts):
    b, s, d = x.shape
    return pl.pallas_call(
        _prenorm_kernel,
        grid=(b, s // ts),
        in_specs=[pl.BlockSpec((1, ts, d), lambda i, t: (i, t, 0)),
                  pl.BlockSpec((1, 3, d), lambda i, t: (i, 0, 0)),
                  pl.BlockSpec((1, d), lambda i, t: (0, 0))],
        out_specs=pl.BlockSpec((1, ts, d), lambda i, t: (i, t, 0)),
        out_shape=jax.ShapeDtypeStruct((b, s, d), _bf16),
        compiler_params=_params(("arbitrary", "arbitrary")),
        name="prenorm",
    )(x, mod3, norm_g)


def _mixer_a_kernel(h_ref, wb_ref, wc_ref, wx_ref, wz_ref, cw_ref, y_ref,
                    carry_ref, ext_ref, *, ts):
    t = pl.program_id(1)
    j = pl.program_id(2)
    h = h_ref[0]

    @pl.when(t == 0)
    def _():
        carry_ref[j] = jnp.zeros(carry_ref.shape[1:], _f32)

    for cols in (slice(c0, c0 + MXU_COLS) for c0 in range(0, y_ref.shape[-1], MXU_COLS)):
        cx = _dot(h, wc_ref[:, cols]) * _dot(h, wx_ref[:, cols])
        ext_ref[0:HALO_A] = carry_ref[j, :, cols]
        ext_ref[HALO_A:] = cx
        carry_ref[j, :, cols] = cx[ts - HALO_A:]
        ext = ext_ref[...]
        w = cw_ref[:, cols]
        conv = w[2:3] * cx
        for d in range(1, CONV_A):
            conv = conv + w[CONV_A - 1 - d:CONV_A - d] * pltpu.roll(ext, d, 0)[HALO_A:]
        y = _dot(h, wb_ref[:, cols]) * conv * jax.nn.silu(_dot(h, wz_ref[:, cols]))
        y_ref[0, :, cols] = y.astype(_bf16)


def _mixer_a(h, w_in, conv_w, w_a, ts, cw):
    b, s, d = h.shape
    nj = w_a // cw

    def wspec(k):
        return pl.BlockSpec((d, cw), lambda i, t, j, k=k: (0, k * nj + j))

    return pl.pallas_call(
        functools.partial(_mixer_a_kernel, ts=ts),
        grid=(b, s // ts, nj),
        in_specs=[pl.BlockSpec((1, ts, d), lambda i, t, j: (i, t, 0)),
                  wspec(0), wspec(1), wspec(2), wspec(3),
                  pl.BlockSpec((CONV_A, cw), lambda i, t, j: (0, j))],
        out_specs=pl.BlockSpec((1, ts, cw), lambda i, t, j: (i, t, j)),
        out_shape=jax.ShapeDtypeStruct((b, s, w_a), _bf16),
        scratch_shapes=[pltpu.VMEM((nj, HALO_A, cw), _f32),
                        pltpu.VMEM((HALO_A + ts, MXU_COLS), _f32)],
        compiler_params=_params(("arbitrary", "arbitrary", "arbitrary")),
        name="mixer_a",
    )(h, w_in, w_in, w_in, w_in, conv_w)


def _mixer_b_kernel(h_ref, wv_ref, wg_ref, wz_ref, cw_ref, cb_ref, lg_ref, lb_ref,
                    y_ref, ext_ref, sh_ref, carry_ref, u_ref, z_ref, k_ref, s1_ref, s2_ref,
                    mu_ref, rstd_ref, *, ts, cw, nj, n_tiles, tiles_per_seq):
    tt = pl.program_id(0)
    j = pl.program_id(1)
    jp = lax.rem(j + nj - 1, nj)
    width = nj * cw

    @pl.when((tt == 0) & (j == 0))
    def _():
        for ref in (ext_ref, carry_ref, u_ref, z_ref, k_ref, s1_ref, s2_ref, mu_ref, rstd_ref):
            ref[...] = jnp.zeros(ref.shape, ref.dtype)

    def conv_block(b):
        first = jp == 0
        nv = (SHIFT_ROWS + HALO_B) // SUBLANES
        x3 = ext_ref[b * SHIFT_ROWS:(b + 1) * SHIFT_ROWS + HALO_B, :].reshape(nv, SUBLANES, cw)
        sh_ref[0] = x3
        sub = lax.broadcasted_iota(jnp.int32, x3.shape, 1)
        for r in range(1, SUBLANES):
            rot = pltpu.roll(x3, r, 1)
            prev = jnp.concatenate([rot[:1], rot[:-1]], axis=0)
            sh_ref[r] = jnp.where(sub < r, prev, rot)
        nvb = ACC_ROWS // SUBLANES
        for rb in range(SHIFT_ROWS // ACC_ROWS):
            accs = []
            n_a = HALO_B // SUBLANES
            for lc in range(cw // LANES):
                lanes = slice(lc * LANES, (lc + 1) * LANES)
                acc = [jnp.zeros((1, SUBLANES, LANES), _f32) + cb_ref[:, lanes] for _ in range(nvb)]
                for r in range(SUBLANES):
                    for v in range(nvb + n_a - 1):
                        t = sh_ref[r, rb * nvb + 1 + v:rb * nvb + 2 + v, :, lanes]
                        for a in range(n_a):
                            d, i = SUBLANES * a + r, v - (n_a - 1) + a
                            if d < CONV_B and 0 <= i < nvb:
                                acc[i] = acc[i] + cw_ref[CONV_B - 1 - d:CONV_B - d, lanes] * t
                accs.append(jnp.concatenate(acc, axis=0))
            row0 = b * SHIFT_ROWS + rb * ACC_ROWS
            rows = slice(row0, row0 + ACC_ROWS)
            u_ref[jp, row0 // SUBLANES:row0 // SUBLANES + nvb] = jnp.concatenate(accs, axis=-1)
            rs = jnp.sum(sum(accs), axis=-1, keepdims=True).reshape(ACC_ROWS, 1)
            k = jnp.where(first, rs * (1.0 / cw), k_ref[rows])
            k3 = k.reshape(nvb, SUBLANES, 1)
            sq = sum((acc - k3) * (acc - k3) for acc in accs)
            s1 = jnp.where(first, 0.0, s1_ref[rows]) + (rs - cw * k)
            s2 = (jnp.where(first, 0.0, s2_ref[rows])
                  + jnp.sum(sq, axis=-1, keepdims=True).reshape(ACC_ROWS, 1))
            k_ref[rows] = k
            s1_ref[rows] = s1
            s2_ref[rows] = s2

    def conv_pending():
        for b in range(ts // SHIFT_ROWS):
            conv_block(b)

    def finish_chunk():
        m1 = s1_ref[...] * (1.0 / width)
        var = s2_ref[...] * (1.0 / width) - m1 * m1
        mu = jnp.where(j == 0, k_ref[...] + m1, mu_ref[...])
        rstd = jnp.where(j == 0, lax.rsqrt(var + EPS), rstd_ref[...])
        mu_ref[...] = mu
        rstd_ref[...] = rstd
        u = u_ref[j].reshape(ts, cw)
        ln = (u - mu) * rstd * lg_ref[...] + lb_ref[...]
        y_ref[0] = (jax.nn.silu(ln) * z_ref[j]).astype(_bf16)

    def mxu_phase():
        h = h_ref[0]
        glu = _dot(h, wv_ref[...]) * jax.nn.sigmoid(_dot(h, wg_ref[...]))
        seq_start = lax.rem(tt, tiles_per_seq) == 0
        ext_ref[0:HALO_B] = jnp.where(seq_start, 0.0, carry_ref[j])
        ext_ref[HALO_B:] = glu
        carry_ref[j] = glu[ts - HALO_B:]
        z_ref[j] = jax.nn.silu(_dot(h, wz_ref[...])).astype(_bf16)

    @pl.when(tt < n_tiles)
    def _():
        conv_pending()
        finish_chunk()
        mxu_phase()

    @pl.when(tt == n_tiles)
    def _():
        pl.when(j == 0)(conv_pending)
        finish_chunk()


def _mixer_b(h, w_in, conv_w, conv_b, ln_g, ln_b, col0, w_b, ts, cw):
    b, s, d = h.shape
    nj = w_b // cw
    j0 = col0 // cw
    tiles_per_seq = s // ts
    n_tiles = b * tiles_per_seq

    def tile(tt):
        return tt // tiles_per_seq, lax.rem(tt, tiles_per_seq)

    def h_map(tt, j):
        return (*tile(jnp.minimum(tt, n_tiles - 1)), 0)

    def y_map(tt, j):
        return (*tile(jnp.maximum(tt - 1, 0)), j * jnp.minimum(tt, 1))

    def wspec(k):
        return pl.BlockSpec((d, cw), lambda tt, j, k=k: (0, j0 + k * nj + j))

    def pending(tt, j):
        return (0, lax.rem(j + nj - 1, nj))

    stat = pltpu.VMEM((ts, 1), _f32)
    return pl.pallas_call(
        functools.partial(_mixer_b_kernel, ts=ts, cw=cw, nj=nj, n_tiles=n_tiles,
                          tiles_per_seq=tiles_per_seq),
        grid=(n_tiles + 1, nj),
        in_specs=[pl.BlockSpec((1, ts, d), h_map),
                  wspec(0), wspec(1), wspec(2),
                  pl.BlockSpec((CONV_B, cw), pending),
                  pl.BlockSpec((1, cw), pending),
                  pl.BlockSpec((1, cw), lambda tt, j: (0, j)),
                  pl.BlockSpec((1, cw), lambda tt, j: (0, j))],
        out_specs=pl.BlockSpec((1, ts, cw), y_map),
        out_shape=jax.ShapeDtypeStruct((b, s, w_b), _bf16),
        scratch_shapes=[pltpu.VMEM((HALO_B + ts, cw), _f32),
                        pltpu.VMEM((SUBLANES, (SHIFT_ROWS + HALO_B) // SUBLANES, SUBLANES, cw), _f32),
                        pltpu.VMEM((nj, HALO_B, cw), _f32),
                        pltpu.VMEM((nj, ts // SUBLANES, SUBLANES, cw), _f32),
                        pltpu.VMEM((nj, ts, cw), _bf16),
                        stat, stat, stat, stat, stat],
        compiler_params=_params(("arbitrary", "arbitrary")),
        name="mixer_b",
    )(h, w_in, w_in, w_in, conv_w, conv_b, ln_g, ln_b)


def _outproj_kernel(ya_ref, yb_ref, w_ref, x_ref, gate_ref, fg_ref, o_ref, *, tn):
    ts, d = o_ref.shape[1:]
    w_a = ya_ref.shape[-1]
    ya = ya_ref[0]
    yb = yb_ref[0]
    ss = jnp.zeros((ts, 1), _f32)
    for k in range(d // tn):
        cols = slice(k * tn, (k + 1) * tn)
        delta = _dot(ya, w_ref[0:w_a, cols]) + _dot(yb, w_ref[w_a:, cols])
        r = x_ref[0, :, cols] + gate_ref[0, :, cols] * delta
        o_ref[0, :, cols] = r
        ss = ss + jnp.sum(r * r, axis=-1, keepdims=True)
    rstd = lax.rsqrt(ss * (1.0 / d) + EPS)
    o_ref[0] = o_ref[0] * rstd * fg_ref[...]


def _outproj(y_a, y_b, w_out, x, gate, final_g, ts, tn):
    b, s, d = x.shape
    w_a = y_a.shape[-1]
    w_b = y_b.shape[-1]
    return pl.pallas_call(
        functools.partial(_outproj_kernel, tn=tn),
        grid=(b, s // ts),
        in_specs=[pl.BlockSpec((1, ts, w_a), lambda i, t: (i, t, 0)),
                  pl.BlockSpec((1, ts, w_b), lambda i, t: (i, t, 0)),
                  pl.BlockSpec((w_a + w_b, d), lambda i, t: (0, 0), pipeline_mode=pl.Buffered(1)),
                  pl.BlockSpec((1, ts, d), lambda i, t: (i, t, 0)),
                  pl.BlockSpec((1, 1, d), lambda i, t: (i, 0, 0)),
                  pl.BlockSpec((1, d), lambda i, t: (0, 0))],
        out_specs=pl.BlockSpec((1, ts, d), lambda i, t: (i, t, 0)),
        out_shape=jax.ShapeDtypeStruct((b, s, d), _f32),
        compiler_params=_params(("arbitrary", "arbitrary")),
        name="outproj",
    )(y_a, y_b, w_out, x, gate, final_g)


def kernel(x, c, norm_g, w_ada, b_ada, w_in, conv_a_w, conv_b_w, conv_b_b, ln_b_g,
           ln_b_b, w_out, final_g):
    batch, seq, d_model = x.shape
    depth = w_ada.shape[0]
    w_a = conv_a_w.shape[-1]
    w_b = conv_b_w.shape[-1]
    assert depth == 1, "the final RMSNorm is fused into the single layer's output projection"
    assert batch <= SUBLANES
    c_pad = jnp.zeros((SUBLANES, d_model), _f32).at[:batch].set(c)
    mod = _ada(c_pad, w_ada[0], b_ada[0][None, :], tn=512)[:batch]
    mod3 = mod.reshape(batch, 3, d_model)
    gate = mod3[:, 2:3, :]
    w_in_bf = w_in[0].astype(_bf16)
    w_out_bf = w_out[0].astype(_bf16)
    h = _prenorm(x, mod3, norm_g[0][None, :], ts=512)
    y_a = _mixer_a(h, w_in_bf, conv_a_w[0], w_a, ts=1024, cw=512)
    y_b = _mixer_b(h, w_in_bf, conv_b_w[0], conv_b_b[0][None, :], ln_b_g[0][None, :],
                   ln_b_b[0][None, :], col0=4 * w_a, w_b=w_b, ts=1024, cw=256)
    return _outproj(y_a, y_b, w_out_bf, x, gate, final_g[None, :], ts=256, tn=512)
```

```python
import functools

import jax
import jax.numpy as jnp
from jax import lax
from jax.experimental import pallas as pl
from jax.experimental.pallas import tpu as pltpu

EPS = 1e-6
CONV_A = 3
CONV_B = 31
SUBLANES = 8
HALO_A = 8
HALO_B = 32
LANES = 128
MXU_COLS = 256
SHIFT_ROWS = 256
ACC_ROWS = 32
VMEM_LIMIT = 60 * 1024 * 1024

_dot = functools.partial(jnp.dot, preferred_element_type=jnp.float32)
_bf16 = jnp.bfloat16
_f32 = jnp.float32


def _params(semantics):
    return pltpu.CompilerParams(dimension_semantics=semantics,
                                vmem_limit_bytes=VMEM_LIMIT)


def _ada_kernel(c_ref, w_ref, b_ref, o_ref):
    c_act = jax.nn.silu(c_ref[...]).astype(_bf16)
    o_ref[...] = _dot(c_act, w_ref[...].astype(_bf16)) + b_ref[...]


def _ada(c_pad, w_ada, b_ada, tn):
    m, d = c_pad.shape
    n = w_ada.shape[1]
    return pl.pallas_call(
        _ada_kernel,
        grid=(n // tn,),
        in_specs=[pl.BlockSpec((m, d), lambda j: (0, 0)),
                  pl.BlockSpec((d, tn), lambda j: (0, j)),
                  pl.BlockSpec((1, tn), lambda j: (0, j))],
        out_specs=pl.BlockSpec((m, tn), lambda j: (0, j)),
        out_shape=jax.ShapeDtypeStruct((m, n), _f32),
        compiler_params=_params(("arbitrary",)),
        name="ada",
    )(c_pad, w_ada, b_ada)


def _prenorm_kernel(x_ref, mod_ref, g_ref, h_ref):
    x = x_ref[0]
    ms = jnp.mean(x * x, axis=-1, keepdims=True)
    y = x * lax.rsqrt(ms + EPS) * g_ref[...]
    shift = mod_ref[0, 0:1, :]
    scale = mod_ref[0, 1:2, :]
    h_ref[0] = (y * (1.0 + scale) + shift).astype(_bf16)


def _prenorm(x, mod3, norm_g, ts):
    b, s, d = x.shape
    return pl.pallas_call(
        _prenorm_kernel,
        grid=(b, s // ts),
        in_specs=[pl.BlockSpec((1, ts, d), lambda i, t: (i, t, 0)),
                  pl.BlockSpec((1, 3, d), lambda i, t: (i, 0, 0)),
                  pl.BlockSpec((1, d), lambda i, t: (0, 0))],
        out_specs=pl.BlockSpec((1, ts, d), lambda i, t: (i, t, 0)),
        out_shape=jax.ShapeDtypeStruct((b, s, d), _bf16),
        compiler_params=_params(("arbitrary", "arbitrary")),
        name="prenorm",
    )(x, mod3, norm_g)


def _mixer_a_kernel(h_ref, wb_ref, wc_ref, wx_ref, wz_ref, cw_ref, y_ref,
                    carry_ref, ext_ref, *, ts):
    t = pl.program_id(1)
    j = pl.program_id(2)
    h = h_ref[0]

    @pl.when(t == 0)
    def _():
        carry_ref[j] = jnp.zeros(carry_ref.shape[1:], _f32)

    for cols in (slice(c0, c0 + MXU_COLS) for c0 in range(0, y_ref.shape[-1], MXU_COLS)):
        cx = _dot(h, wc_ref[:, cols]) * _dot(h, wx_ref[:, cols])
        ext_ref[0:HALO_A] = carry_ref[j, :, cols]
        ext_ref[HALO_A:] = cx
        carry_ref[j, :, cols] = cx[ts - HALO_A:]
        ext = ext_ref[...]
        w = cw_ref[:, cols]
        conv = w[2:3] * cx
        for d in range(1, CONV_A):
            conv = conv + w[CONV_A - 1 - d:CONV_A - d] * pltpu.roll(ext, d, 0)[HALO_A:]
        y = _dot(h, wb_ref[:, cols]) * conv * jax.nn.silu(_dot(h, wz_ref[:, cols]))
        y_ref[0, :, cols] = y.astype(_bf16)


def _mixer_a(h, w_in, conv_w, w_a, ts, cw):
    b, s, d = h.shape
    nj = w_a // cw

    def wspec(k):
        return pl.BlockSpec((d, cw), lambda i, t, j, k=k: (0, k * nj + j))

    return pl.pallas_call(
        functools.partial(_mixer_a_kernel, ts=ts),
        grid=(b, s // ts, nj),
        in_specs=[pl.BlockSpec((1, ts, d), lambda i, t, j: (i, t, 0)),
                  wspec(0), wspec(1), wspec(2), wspec(3),
                  pl.BlockSpec((CONV_A, cw), lambda i, t, j: (0, j))],
        out_specs=pl.BlockSpec((1, ts, cw), lambda i, t, j: (i, t, j)),
        out_shape=jax.ShapeDtypeStruct((b, s, w_a), _bf16),
        scratch_shapes=[pltpu.VMEM((nj, HALO_A, cw), _f32),
                        pltpu.VMEM((HALO_A + ts, MXU_COLS), _f32)],
        compiler_params=_params(("arbitrary", "arbitrary", "arbitrary")),
        name="mixer_a",
    )(h, w_in, w_in, w_in, w_in, conv_w)


def _mixer_b_kernel(h_ref, wv_ref, wg_ref, wz_ref, cw_ref, cb_ref, lg_ref, lb_ref,
                    y_ref, ext_ref, sh_ref, carry_ref, u_ref, z_ref, k_ref, s1_ref, s2_ref,
                    mu_ref, rstd_ref, *, ts, cw, nj, n_tiles, tiles_per_seq):
    tt = pl.program_id(0)
    j = pl.program_id(1)
    jp = lax.rem(j + nj - 1, nj)
    width = nj * cw

    @pl.when((tt == 0) & (j == 0))
    def _():
        for ref in (ext_ref, carry_ref, u_ref, z_ref, k_ref, s1_ref, s2_ref, mu_ref, rstd_ref):
            ref[...] = jnp.zeros(ref.shape, ref.dtype)

    def conv_block(b):
        first = jp == 0
        nv = (SHIFT_ROWS + HALO_B) // SUBLANES
        x3 = ext_ref[b * SHIFT_ROWS:(b + 1) * SHIFT_ROWS + HALO_B, :].reshape(nv, SUBLANES, cw)
        sh_ref[0] = x3
        sub = lax.broadcasted_iota(jnp.int32, x3.shape, 1)
        for r in range(1, SUBLANES):
            rot = pltpu.roll(x3, r, 1)
            prev = jnp.concatenate([rot[:1], rot[:-1]], axis=0)
            sh_ref[r] = jnp.where(sub < r, prev, rot)
        nvb = ACC_ROWS // SUBLANES
        for rb in range(SHIFT_ROWS // ACC_ROWS):
            accs = []
            n_a = HALO_B // SUBLANES
            for lc in range(cw // LANES):
                lanes = slice(lc * LANES, (lc + 1) * LANES)
                acc = [jnp.zeros((1, SUBLANES, LANES), _f32) + cb_ref[:, lanes] for _ in range(nvb)]
                for r in range(SUBLANES):
                    for v in range(nvb + n_a - 1):
                        t = sh_ref[r, rb * nvb + 1 + v:rb * nvb + 2 + v, :, lanes]
                        for a in range(n_a):
                            d, i = SUBLANES * a + r, v - (n_a - 1) + a
                            if d < CONV_B and 0 <= i < nvb:
                                acc[i] = acc[i] + cw_ref[CONV_B - 1 - d:CONV_B - d, lanes] * t
                accs.append(jnp.concatenate(acc, axis=0))
            row0 = b * SHIFT_ROWS + rb * ACC_ROWS
            rows = slice(row0, row0 + ACC_ROWS)
            u_ref[jp, row0 // SUBLANES:row0 // SUBLANES + nvb] = jnp.concatenate(accs, axis=-1)
            rs = jnp.sum(sum(accs), axis=-1, keepdims=True).reshape(ACC_ROWS, 1)
            k = jnp.where(first, rs * (1.0 / cw), k_ref[rows])
            k3 = k.reshape(nvb, SUBLANES, 1)
            sq = sum((acc - k3) * (acc - k3) for acc in accs)
            s1 = jnp.where(first, 0.0, s1_ref[rows]) + (rs - cw * k)
            s2 = (jnp.where(first, 0.0, s2_ref[rows])
                  + jnp.sum(sq, axis=-1, keepdims=True).reshape(ACC_ROWS, 1))
            k_ref[rows] = k
            s1_ref[rows] = s1
            s2_ref[rows] = s2

    def conv_pending():
        for b in range(ts // SHIFT_ROWS):
            conv_block(b)

    def finish_chunk():
        m1 = s1_ref[...] * (1.0 / width)
        var = s2_ref[...] * (1.0 / width) - m1 * m1
        mu = jnp.where(j == 0, k_ref[...] + m1, mu_ref[...])
        rstd = jnp.where(j == 0, lax.rsqrt(var + EPS), rstd_ref[...])
        mu_ref[...] = mu
        rstd_ref[...] = rstd
        u = u_ref[j].reshape(ts, cw)
        ln = (u - mu) * rstd * lg_ref[...] + lb_ref[...]
        y_ref[0] = (jax.nn.silu(ln) * z_ref[j]).astype(_bf16)

    def mxu_phase():
        h = h_ref[0]
        z_ref[j] = jax.nn.silu(_dot(h, wz_ref[...])).astype(_bf16)
        gate = jax.nn.sigmoid(_dot(h, wg_ref[...]))
        glu = _dot(h, wv_ref[...]) * gate
        seq_start = lax.rem(tt, tiles_per_seq) == 0
        ext_ref[0:HALO_B] = jnp.where(seq_start, 0.0, carry_ref[j])
        ext_ref[HALO_B:] = glu
        carry_ref[j] = glu[ts - HALO_B:]

    @pl.when(tt < n_tiles)
    def _():
        conv_pending()
        finish_chunk()
        mxu_phase()

    @pl.when(tt == n_tiles)
    def _():
        pl.when(j == 0)(conv_pending)
        finish_chunk()


def _mixer_b(h, w_in, conv_w, conv_b, ln_g, ln_b, col0, w_b, ts, cw):
    b, s, d = h.shape
    nj = w_b // cw
    j0 = col0 // cw
    tiles_per_seq = s // ts
    n_tiles = b * tiles_per_seq

    def tile(tt):
        return tt // tiles_per_seq, lax.rem(tt, tiles_per_seq)

    def h_map(tt, j):
        return (*tile(jnp.minimum(tt, n_tiles - 1)), 0)

    def y_map(tt, j):
        return (*tile(jnp.maximum(tt - 1, 0)), j * jnp.minimum(tt, 1))

    def wspec(k):
        return pl.BlockSpec((d, cw), lambda tt, j, k=k: (0, j0 + k * nj + j))

    def pending(tt, j):
        return (0, lax.rem(j + nj - 1, nj))

    stat = pltpu.VMEM((ts, 1), _f32)
    return pl.pallas_call(
        functools.partial(_mixer_b_kernel, ts=ts, cw=cw, nj=nj, n_tiles=n_tiles,
                          tiles_per_seq=tiles_per_seq),
        grid=(n_tiles + 1, nj),
        in_specs=[pl.BlockSpec((1, ts, d), h_map),
                  wspec(0), wspec(1), wspec(2),
                  pl.BlockSpec((CONV_B, cw), pending),
                  pl.BlockSpec((1, cw), pending),
                  pl.BlockSpec((1, cw), lambda tt, j: (0, j)),
                  pl.BlockSpec((1, cw), lambda tt, j: (0, j))],
        out_specs=pl.BlockSpec((1, ts, cw), y_map),
        out_shape=jax.ShapeDtypeStruct((b, s, w_b), _bf16),
        scratch_shapes=[pltpu.VMEM((HALO_B + ts, cw), _f32),
                        pltpu.VMEM((SUBLANES, (SHIFT_ROWS + HALO_B) // SUBLANES, SUBLANES, cw), _f32),
                        pltpu.VMEM((nj, HALO_B, cw), _f32),
                        pltpu.VMEM((nj, ts // SUBLANES, SUBLANES, cw), _f32),
                        pltpu.VMEM((nj, ts, cw), _bf16),
                        stat, stat, stat, stat, stat],
        compiler_params=_params(("arbitrary", "arbitrary")),
        name="mixer_b",
    )(h, w_in, w_in, w_in, conv_w, conv_b, ln_g, ln_b)


def _outproj_kernel(ya_ref, yb_ref, w_ref, x_ref, gate_ref, fg_ref, o_ref, *, tn):
    ts, d = o_ref.shape[1:]
    w_a = ya_ref.shape[-1]
    ya = ya_ref[0]
    yb = yb_ref[0]
    ss = jnp.zeros((ts, 1), _f32)
    for k in range(d // tn):
        cols = slice(k * tn, (k + 1) * tn)
        delta = _dot(ya, w_ref[0:w_a, cols]) + _dot(yb, w_ref[w_a:, cols])
        r = x_ref[0, :, cols] + gate_ref[0, :, cols] * delta
        o_ref[0, :, cols] = r
        ss = ss + jnp.sum(r * r, axis=-1, keepdims=True)
    rstd = lax.rsqrt(ss * (1.0 / d) + EPS)
    o_ref[0] = o_ref[0] * rstd * fg_ref[...]


def _outproj(y_a, y_b, w_out, x, gate, final_g, ts, tn):
    b, s, d = x.shape
    w_a = y_a.shape[-1]
    w_b = y_b.shape[-1]
    return pl.pallas_call(
        functools.partial(_outproj_kernel, tn=tn),
        grid=(b, s // ts),
        in_specs=[pl.BlockSpec((1, ts, w_a), lambda i, t: (i, t, 0)),
                  pl.BlockSpec((1, ts, w_b), lambda i, t: (i, t, 0)),
                  pl.BlockSpec((w_a + w_b, d), lambda i, t: (0, 0), pipeline_mode=pl.Buffered(1)),
                  pl.BlockSpec((1, ts, d), lambda i, t: (i, t, 0)),
                  pl.BlockSpec((1, 1, d), lambda i, t: (i, 0, 0)),
                  pl.BlockSpec((1, d), lambda i, t: (0, 0))],
        out_specs=pl.BlockSpec((1, ts, d), lambda i, t: (i, t, 0)),
        out_shape=jax.ShapeDtypeStruct((b, s, d), _f32),
        compiler_params=_params(("arbitrary", "arbitrary")),
        name="outproj",
    )(y_a, y_b, w_out, x, gate, final_g)


def kernel(x, c, norm_g, w_ada, b_ada, w_in, conv_a_w, conv_b_w, conv_b_b, ln_b_g,
           ln_b_b, w_out, final_g):
    batch, seq, d_model = x.shape
    depth = w_ada.shape[0]
    w_a = conv_a_w.shape[-1]
    w_b = conv_b_w.shape[-1]
    assert depth == 1, "the final RMSNorm is fused into the single layer's output projection"
    assert batch <= SUBLANES
    c_pad = jnp.zeros((SUBLANES, d_model), _f32).at[:batch].set(c)
    mod = _ada(c_pad, w_ada[0], b_ada[0][None, :], tn=512)[:batch]
    mod3 = mod.reshape(batch, 3, d_model)
    gate = mod3[:, 2:3, :]
    w_in_bf = w_in[0].astype(_bf16)
    w_out_bf = w_out[0].astype(_bf16)
    h = _prenorm(x, mod3, norm_g[0][None, :], ts=512)
    y_a = _mixer_a(h, w_in_bf, conv_a_w[0], w_a, ts=1024, cw=512)
    y_b = _mixer_b(h, w_in_bf, conv_b_w[0], conv_b_b[0][None, :], ln_b_g[0][None, :],
                   ln_b_b[0][None, :], col0=4 * w_a, w_b=w_b, ts=1024, cw=256)
    return _outproj(y_a, y_b, w_out_bf, x, gate, final_g[None, :], ts=256, tn=512)
```

```python
import functools

import jax
import jax.numpy as jnp
from jax import lax
from jax.experimental import pallas as pl
from jax.experimental.pallas import tpu as pltpu

EPS = 1e-6
CONV_A = 3
CONV_B = 31
SUBLANES = 8
HALO_A = 8
HALO_B = 32
LANES = 128
MXU_COLS = 256
PRENORM_ROWS = 16
SHIFT_ROWS = 256
ACC_ROWS = 32
VMEM_LIMIT = 60 * 1024 * 1024

_dot = functools.partial(jnp.dot, preferred_element_type=jnp.float32)
_bf16 = jnp.bfloat16
_f32 = jnp.float32


def _params(semantics):
    return pltpu.CompilerParams(dimension_semantics=semantics,
                                vmem_limit_bytes=VMEM_LIMIT)


def _ada_kernel(c_ref, w_ref, b_ref, o_ref):
    c_act = jax.nn.silu(c_ref[...]).astype(_bf16)
    o_ref[...] = _dot(c_act, w_ref[...].astype(_bf16)) + b_ref[...]


def _ada(c_pad, w_ada, b_ada, tn):
    m, d = c_pad.shape
    n = w_ada.shape[1]
    return pl.pallas_call(
        _ada_kernel,
        grid=(n // tn,),
        in_specs=[pl.BlockSpec((m, d), lambda j: (0, 0)),
                  pl.BlockSpec((d, tn), lambda j: (0, j)),
                  pl.BlockSpec((1, tn), lambda j: (0, j))],
        out_specs=pl.BlockSpec((m, tn), lambda j: (0, j)),
        out_shape=jax.ShapeDtypeStruct((m, n), _f32),
        compiler_params=_params(("arbitrary",)),
        name="ada",
    )(c_pad, w_ada, b_ada)


def _prenorm_kernel(x_ref, mod_ref, g_ref, h_ref):
    shift = mod_ref[0, 0:1, :]
    gain = g_ref[...] * (1.0 + mod_ref[0, 1:2, :])
    for r0 in range(0, x_ref.shape[1], PRENORM_ROWS):
        x = x_ref[0, r0:r0 + PRENORM_ROWS, :]
        ms = jnp.mean(x * x, axis=-1, keepdims=True)
        h_ref[0, r0:r0 + PRENORM_ROWS, :] = (x * lax.rsqrt(ms + EPS) * gain + shift).astype(_bf16)


def _prenorm(x, mod3, norm_g, ts):
    b, s, d = x.shape
    return pl.pallas_call(
        _prenorm_kernel,
        grid=(b, s // ts),
        in_specs=[pl.BlockSpec((1, ts, d), lambda i, t: (i, t, 0)),
                  pl.BlockSpec((1, 3, d), lambda i, t: (i, 0, 0)),
                  pl.BlockSpec((1, d), lambda i, t: (0, 0))],
        out_specs=pl.BlockSpec((1, ts, d), lambda i, t: (i, t, 0)),
        out_shape=jax.ShapeDtypeStruct((b, s, d), _bf16),
        compiler_params=_params(("arbitrary", "arbitrary")),
        name="prenorm",
    )(x, mod3, norm_g)


def _mixer_a_kernel(h_ref, wb_ref, wc_ref, wx_ref, wz_ref, cw_ref, y_ref,
                    carry_ref, ext_ref, *, ts):
    t = pl.program_id(1)
    j = pl.program_id(2)
    h = h_ref[0]

    @pl.when(t == 0)
    def _():
        carry_ref[j] = jnp.zeros(carry_ref.shape[1:], _f32)

    for cols in (slice(c0, c0 + MXU_COLS) for c0 in range(0, y_ref.shape[-1], MXU_COLS)):
        cx = _dot(h, wc_ref[:, cols]) * _dot(h, wx_ref[:, cols])
        ext_ref[0:HALO_A] = carry_ref[j, :, cols]
        ext_ref[HALO_A:] = cx
        carry_ref[j, :, cols] = cx[ts - HALO_A:]
        ext = ext_ref[...]
        w = cw_ref[:, cols]
        conv = w[2:3] * cx
        for d in range(1, CONV_A):
            conv = conv + w[CONV_A - 1 - d:CONV_A - d] * pltpu.roll(ext, d, 0)[HALO_A:]
        y = _dot(h, wb_ref[:, cols]) * conv * jax.nn.silu(_dot(h, wz_ref[:, cols]))
        y_ref[0, :, cols] = y.astype(_bf16)


def _mixer_a(h, w_in, conv_w, w_a, ts, cw):
    b, s, d = h.shape
    nj = w_a // cw

    def wspec(k):
        return pl.BlockSpec((d, cw), lambda i, t, j, k=k: (0, k * nj + j))

    return pl.pallas_call(
        functools.partial(_mixer_a_kernel, ts=ts),
        grid=(b, s // ts, nj),
        in_specs=[pl.BlockSpec((1, ts, d), lambda i, t, j: (i, t, 0)),
                  wspec(0), wspec(1), wspec(2), wspec(3),
                  pl.BlockSpec((CONV_A, cw), lambda i, t, j: (0, j))],
        out_specs=pl.BlockSpec((1, ts, cw), lambda i, t, j: (i, t, j)),
        out_shape=jax.ShapeDtypeStruct((b, s, w_a), _bf16),
        scratch_shapes=[pltpu.VMEM((nj, HALO_A, cw), _f32),
                        pltpu.VMEM((HALO_A + ts, MXU_COLS), _f32)],
        compiler_params=_params(("arbitrary", "arbitrary", "arbitrary")),
        name="mixer_a",
    )(h, w_in, w_in, w_in, w_in, conv_w)


def _mixer_b_kernel(h_ref, wv_ref, wg_ref, wz_ref, cw_ref, cb_ref, lg_ref, lb_ref,
                    y_ref, ext_ref, sh_ref, carry_ref, u_ref, z_ref, k_ref, s1_ref, s2_ref,
                    mu_ref, rstd_ref, *, ts, cw, nj, n_tiles, tiles_per_seq):
    tt = pl.program_id(0)
    j = pl.program_id(1)
    jp = lax.rem(j + nj - 1, nj)
    width = nj * cw

    @pl.when((tt == 0) & (j == 0))
    def _():
        for ref in (ext_ref, carry_ref, u_ref, z_ref, k_ref, s1_ref, s2_ref, mu_ref, rstd_ref):
            ref[...] = jnp.zeros(ref.shape, ref.dtype)

    def conv_block(b):
        first = jp == 0
        nv = (SHIFT_ROWS + HALO_B) // SUBLANES
        x3 = ext_ref[b * SHIFT_ROWS:(b + 1) * SHIFT_ROWS + HALO_B, :].reshape(nv, SUBLANES, cw)
        sh_ref[0] = x3
        sub = lax.broadcasted_iota(jnp.int32, x3.shape, 1)
        for r in range(1, SUBLANES):
            rot = pltpu.roll(x3, r, 1)
            prev = jnp.concatenate([rot[:1], rot[:-1]], axis=0)
            sh_ref[r] = jnp.where(sub < r, prev, rot)
        nvb = ACC_ROWS // SUBLANES
        for rb in range(SHIFT_ROWS // ACC_ROWS):
            accs = []
            n_a = HALO_B // SUBLANES
            for lc in range(cw // LANES):
                lanes = slice(lc * LANES, (lc + 1) * LANES)
                acc = [jnp.zeros((1, SUBLANES, LANES), _f32) + cb_ref[:, lanes] for _ in range(nvb)]
                for r in range(SUBLANES):
                    for v in range(nvb + n_a - 1):
                        t = sh_ref[r, rb * nvb + 1 + v:rb * nvb + 2 + v, :, lanes]
                        for a in range(n_a):
                            d, i = SUBLANES * a + r, v - (n_a - 1) + a
                            if d < CONV_B and 0 <= i < nvb:
                                acc[i] = acc[i] + cw_ref[CONV_B - 1 - d:CONV_B - d, lanes] * t
                accs.append(jnp.concatenate(acc, axis=0))
            row0 = b * SHIFT_ROWS + rb * ACC_ROWS
            rows = slice(row0, row0 + ACC_ROWS)
            u_ref[jp, row0 // SUBLANES:row0 // SUBLANES + nvb] = jnp.concatenate(accs, axis=-1)
            rs = jnp.sum(sum(accs), axis=-1, keepdims=True).reshape(ACC_ROWS, 1)
            k = jnp.where(first, rs * (1.0 / cw), k_ref[rows])
            k3 = k.reshape(nvb, SUBLANES, 1)
            sq = sum((acc - k3) * (acc - k3) for acc in accs)
            s1 = jnp.where(first, 0.0, s1_ref[rows]) + (rs - cw * k)
            s2 = (jnp.where(first, 0.0, s2_ref[rows])
                  + jnp.sum(sq, axis=-1, keepdims=True).reshape(ACC_ROWS, 1))
            k_ref[rows] = k
            s1_ref[rows] = s1
            s2_ref[rows] = s2

    def conv_pending():
        for b in range(ts // SHIFT_ROWS):
            conv_block(b)

    def finish_chunk():
        m1 = s1_ref[...] * (1.0 / width)
        var = s2_ref[...] * (1.0 / width) - m1 * m1
        mu = jnp.where(j == 0, k_ref[...] + m1, mu_ref[...])
        rstd = jnp.where(j == 0, lax.rsqrt(var + EPS), rstd_ref[...])
        mu_ref[...] = mu
        rstd_ref[...] = rstd
        u = u_ref[j].reshape(ts, cw)
        ln = (u - mu) * rstd * lg_ref[...] + lb_ref[...]
        y_ref[0] = (jax.nn.silu(ln) * z_ref[j]).astype(_bf16)

    def mxu_phase():
        h = h_ref[0]
        z_ref[j] = jax.nn.silu(_dot(h, wz_ref[...])).astype(_bf16)
        gate = jax.nn.sigmoid(_dot(h, wg_ref[...]))
        glu = _dot(h, wv_ref[...]) * gate
        seq_start = lax.rem(tt, tiles_per_seq) == 0
        ext_ref[0:HALO_B] = jnp.where(seq_start, 0.0, carry_ref[j])
        ext_ref[HALO_B:] = glu
        carry_ref[j] = glu[ts - HALO_B:]

    @pl.when(tt < n_tiles)
    def _():
        conv_pending()
        finish_chunk()
        mxu_phase()

    @pl.when(tt == n_tiles)
    def _():
        pl.when(j == 0)(conv_pending)
        finish_chunk()


def _mixer_b(h, w_in, conv_w, conv_b, ln_g, ln_b, col0, w_b, ts, cw):
    b, s, d = h.shape
    nj = w_b // cw
    j0 = col0 // cw
    tiles_per_seq = s // ts
    n_tiles = b * tiles_per_seq

    def tile(tt):
        return tt // tiles_per_seq, lax.rem(tt, tiles_per_seq)

    def h_map(tt, j):
        return (*tile(jnp.minimum(tt, n_tiles - 1)), 0)

    def y_map(tt, j):
        return (*tile(jnp.maximum(tt - 1, 0)), j * jnp.minimum(tt, 1))

    def wspec(k):
        return pl.BlockSpec((d, cw), lambda tt, j, k=k: (0, j0 + k * nj + j))

    def pending(tt, j):
        return (0, lax.rem(j + nj - 1, nj))

    stat = pltpu.VMEM((ts, 1), _f32)
    return pl.pallas_call(
        functools.partial(_mixer_b_kernel, ts=ts, cw=cw, nj=nj, n_tiles=n_tiles,
                          tiles_per_seq=tiles_per_seq),
        grid=(n_tiles + 1, nj),
        in_specs=[pl.BlockSpec((1, ts, d), h_map),
                  wspec(0), wspec(1), wspec(2),
                  pl.BlockSpec((CONV_B, cw), pending),
                  pl.BlockSpec((1, cw), pending),
                  pl.BlockSpec((1, cw), lambda tt, j: (0, j)),
                  pl.BlockSpec((1, cw), lambda tt, j: (0, j))],
        out_specs=pl.BlockSpec((1, ts, cw), y_map),
        out_shape=jax.ShapeDtypeStruct((b, s, w_b), _bf16),
        scratch_shapes=[pltpu.VMEM((HALO_B + ts, cw), _f32),
                        pltpu.VMEM((SUBLANES, (SHIFT_ROWS + HALO_B) // SUBLANES, SUBLANES, cw), _f32),
                        pltpu.VMEM((nj, HALO_B, cw), _f32),
                        pltpu.VMEM((nj, ts // SUBLANES, SUBLANES, cw), _f32),
                        pltpu.VMEM((nj, ts, cw), _bf16),
                        stat, stat, stat, stat, stat],
        compiler_params=_params(("arbitrary", "arbitrary")),
        name="mixer_b",
    )(h, w_in, w_in, w_in, conv_w, conv_b, ln_g, ln_b)


def _outproj_kernel(ya_ref, yb_ref, w_ref, x_ref, gate_ref, fg_ref, o_ref, *, tn):
    ts, d = o_ref.shape[1:]
    w_a = ya_ref.shape[-1]
    ya = ya_ref[0]
    yb = yb_ref[0]
    ss = jnp.zeros((ts, 1), _f32)
    for k in range(d // tn):
        cols = slice(k * tn, (k + 1) * tn)
        delta = _dot(ya, w_ref[0:w_a, cols]) + _dot(yb, w_ref[w_a:, cols])
        r = x_ref[0, :, cols] + gate_ref[0, :, cols] * delta
        o_ref[0, :, cols] = r
        ss = ss + jnp.sum(r * r, axis=-1, keepdims=True)
    rstd = lax.rsqrt(ss * (1.0 / d) + EPS)
    o_ref[0] = o_ref[0] * rstd * fg_ref[...]


def _outproj(y_a, y_b, w_out, x, gate, final_g, ts, tn):
    b, s, d = x.shape
    w_a = y_a.shape[-1]
    w_b = y_b.shape[-1]
    return pl.pallas_call(
        functools.partial(_outproj_kernel, tn=tn),
        grid=(b, s // ts),
        in_specs=[pl.BlockSpec((1, ts, w_a), lambda i, t: (i, t, 0)),
                  pl.BlockSpec((1, ts, w_b), lambda i, t: (i, t, 0)),
                  pl.BlockSpec((w_a + w_b, d), lambda i, t: (0, 0), pipeline_mode=pl.Buffered(1)),
                  pl.BlockSpec((1, ts, d), lambda i, t: (i, t, 0)),
                  pl.BlockSpec((1, 1, d), lambda i, t: (i, 0, 0)),
                  pl.BlockSpec((1, d), lambda i, t: (0, 0))],
        out_specs=pl.BlockSpec((1, ts, d), lambda i, t: (i, t, 0)),
        out_shape=jax.ShapeDtypeStruct((b, s, d), _f32),
        compiler_params=_params(("arbitrary", "arbitrary")),
        name="outproj",
    )(y_a, y_b, w_out, x, gate, final_g)


def kernel(x, c, norm_g, w_ada, b_ada, w_in, conv_a_w, conv_b_w, conv_b_b, ln_b_g,
           ln_b_b, w_out, final_g):
    batch, seq, d_model = x.shape
    depth = w_ada.shape[0]
    w_a = conv_a_w.shape[-1]
    w_b = conv_b_w.shape[-1]
    assert depth == 1, "the final RMSNorm is fused into the single layer's output projection"
    assert batch <= SUBLANES
    c_pad = jnp.zeros((SUBLANES, d_model), _f32).at[:batch].set(c)
    mod = _ada(c_pad, w_ada[0], b_ada[0][None, :], tn=512)[:batch]
    mod3 = mod.reshape(batch, 3, d_model)
    gate = mod3[:, 2:3, :]
    w_in_bf = w_in[0].astype(_bf16)
    w_out_bf = w_out[0].astype(_bf16)
    h = _prenorm(x, mod3, norm_g[0][None, :], ts=512)
    y_a = _mixer_a(h, w_in_bf, conv_a_w[0], w_a, ts=1024, cw=512)
    y_b = _mixer_b(h, w_in_bf, conv_b_w[0], conv_b_b[0][None, :], ln_b_g[0][None, :],
                   ln_b_b[0][None, :], col0=4 * w_a, w_b=w_b, ts=1024, cw=256)
    return _outproj(y_a, y_b, w_out_bf, x, gate, final_g[None, :], ts=256, tn=512)
```

```python
import functools

import jax
import jax.numpy as jnp
from jax import lax
from jax.experimental import pallas as pl
from jax.experimental.pallas import tpu as pltpu

EPS = 1e-6
CONV_A = 3
CONV_B = 31
SUBLANES = 8
HALO_A = 8
HALO_B = 32
LANES = 128
MXU_COLS = 256
PRENORM_ROWS = 16
EPI_ROWS = 32
SHIFT_ROWS = 256
ACC_ROWS = 32
VMEM_LIMIT = 60 * 1024 * 1024

_dot = functools.partial(jnp.dot, preferred_element_type=jnp.float32)
_bf16 = jnp.bfloat16
_f32 = jnp.float32


def _params(semantics):
    return pltpu.CompilerParams(dimension_semantics=semantics,
                                vmem_limit_bytes=VMEM_LIMIT)


def _ada_kernel(c_ref, w_ref, b_ref, o_ref):
    c_act = jax.nn.silu(c_ref[...]).astype(_bf16)
    o_ref[...] = _dot(c_act, w_ref[...].astype(_bf16)) + b_ref[...]


def _ada(c_pad, w_ada, b_ada, tn):
    m, d = c_pad.shape
    n = w_ada.shape[1]
    return pl.pallas_call(
        _ada_kernel,
        grid=(n // tn,),
        in_specs=[pl.BlockSpec((m, d), lambda j: (0, 0)),
                  pl.BlockSpec((d, tn), lambda j: (0, j)),
                  pl.BlockSpec((1, tn), lambda j: (0, j))],
        out_specs=pl.BlockSpec((m, tn), lambda j: (0, j)),
        out_shape=jax.ShapeDtypeStruct((m, n), _f32),
        compiler_params=_params(("arbitrary",)),
        name="ada",
    )(c_pad, w_ada, b_ada)


def _prenorm_kernel(x_ref, mod_ref, g_ref, h_ref):
    shift = mod_ref[0, 0:1, :]
    gain = g_ref[...] * (1.0 + mod_ref[0, 1:2, :])
    for r0 in range(0, x_ref.shape[1], PRENORM_ROWS):
        x = x_ref[0, r0:r0 + PRENORM_ROWS, :]
        ms = jnp.mean(x * x, axis=-1, keepdims=True)
        h_ref[0, r0:r0 + PRENORM_ROWS, :] = (x * lax.rsqrt(ms + EPS) * gain + shift).astype(_bf16)


def _prenorm(x, mod3, norm_g, ts):
    b, s, d = x.shape
    return pl.pallas_call(
        _prenorm_kernel,
        grid=(b, s // ts),
        in_specs=[pl.BlockSpec((1, ts, d), lambda i, t: (i, t, 0)),
                  pl.BlockSpec((1, 3, d), lambda i, t: (i, 0, 0)),
                  pl.BlockSpec((1, d), lambda i, t: (0, 0))],
        out_specs=pl.BlockSpec((1, ts, d), lambda i, t: (i, t, 0)),
        out_shape=jax.ShapeDtypeStruct((b, s, d), _bf16),
        compiler_params=_params(("arbitrary", "arbitrary")),
        name="prenorm",
    )(x, mod3, norm_g)


def _mixer_a_kernel(h_ref, wb_ref, wc_ref, wx_ref, wz_ref, cw_ref, y_ref,
                    carry_ref, ext_ref, b_ref, sz_ref, *, ts, n_tiles, tiles_per_seq):
    tt = pl.program_id(0)
    j = pl.program_id(1)
    halves = [slice(c0, c0 + MXU_COLS) for c0 in range(0, y_ref.shape[-1], MXU_COLS)]

    @pl.when((tt == 0) & (j == 0))
    def _():
        for ref in (carry_ref, ext_ref, b_ref, sz_ref):
            ref[...] = jnp.zeros(ref.shape, ref.dtype)

    def finish_pending():
        nv = (EPI_ROWS + HALO_A) // SUBLANES
        sub = lax.broadcasted_iota(jnp.int32, (nv, SUBLANES, MXU_COLS), 1)
        for cols in halves:
            w = cw_ref[:, cols]
            for r0 in range(0, ts, EPI_ROWS):
                x3 = ext_ref[r0:r0 + EPI_ROWS + HALO_A, cols].reshape(nv, SUBLANES, MXU_COLS)
                conv = w[2:3] * x3[1:]
                for d in range(1, CONV_A):
                    rot = pltpu.roll(x3, d, 1)
                    prev = jnp.concatenate([rot[:1], rot[:-1]], axis=0)
                    conv = conv + w[CONV_A - 1 - d:CONV_A - d] * jnp.where(sub < d, prev, rot)[1:]
                rows = slice(r0, r0 + EPI_ROWS)
                y = b_ref[rows, cols] * conv.reshape(EPI_ROWS, MXU_COLS) * sz_ref[rows, cols]
                y_ref[0, rows, cols] = y.astype(_bf16)

    def project():
        h = h_ref[0]
        seq_start = lax.rem(tt, tiles_per_seq) == 0
        for cols in halves:
            cx = _dot(h, wc_ref[:, cols]) * _dot(h, wx_ref[:, cols])
            ext_ref[0:HALO_A, cols] = jnp.where(seq_start, 0.0, carry_ref[j, :, cols])
            ext_ref[HALO_A:, cols] = cx
            carry_ref[j, :, cols] = cx[ts - HALO_A:]
            b_ref[:, cols] = _dot(h, wb_ref[:, cols])
            sz_ref[:, cols] = jax.nn.silu(_dot(h, wz_ref[:, cols])).astype(_bf16)

    @pl.when(tt < n_tiles)
    def _():
        finish_pending()
        project()

    @pl.when((tt == n_tiles) & (j == 0))
    def _():
        finish_pending()


def _mixer_a(h, w_in, conv_w, w_a, ts, cw):
    b, s, d = h.shape
    nj = w_a // cw
    tiles_per_seq = s // ts
    n_tiles = b * tiles_per_seq
    n_steps = n_tiles * nj

    def tile(tt):
        return tt // tiles_per_seq, lax.rem(tt, tiles_per_seq)

    def chunk(tt, j):
        return jnp.where(tt < n_tiles, j, nj - 1)

    def y_map(tt, j):
        m = jnp.clip(tt * nj + j - 1, 0, n_steps - 1)
        return (*tile(m // nj), lax.rem(m, nj))

    def wspec(k):
        return pl.BlockSpec((d, cw), lambda tt, j, k=k: (0, k * nj + chunk(tt, j)))

    return pl.pallas_call(
        functools.partial(_mixer_a_kernel, ts=ts, n_tiles=n_tiles, tiles_per_seq=tiles_per_seq),
        grid=(n_tiles + 1, nj),
        in_specs=[pl.BlockSpec((1, ts, d), lambda tt, j: (*tile(jnp.minimum(tt, n_tiles - 1)), 0)),
                  wspec(0), wspec(1), wspec(2), wspec(3),
                  pl.BlockSpec((CONV_A, cw), lambda tt, j: (0, y_map(tt, j)[2]))],
        out_specs=pl.BlockSpec((1, ts, cw), y_map),
        out_shape=jax.ShapeDtypeStruct((b, s, w_a), _bf16),
        scratch_shapes=[pltpu.VMEM((nj, HALO_A, cw), _f32),
                        pltpu.VMEM((HALO_A + ts, cw), _f32),
                        pltpu.VMEM((ts, cw), _f32),
                        pltpu.VMEM((ts, cw), _bf16)],
        compiler_params=_params(("arbitrary", "arbitrary")),
        name="mixer_a",
    )(h, w_in, w_in, w_in, w_in, conv_w)


def _mixer_b_kernel(h_ref, wv_ref, wg_ref, wz_ref, cw_ref, cb_ref, lg_ref, lb_ref,
                    y_ref, ext_ref, sh_ref, carry_ref, u_ref, z_ref, k_ref, s1_ref, s2_ref,
                    mu_ref, rstd_ref, *, ts, cw, nj, n_tiles, tiles_per_seq):
    tt = pl.program_id(0)
    j = pl.program_id(1)
    jp = lax.rem(j + nj - 1, nj)
    width = nj * cw

    @pl.when((tt == 0) & (j == 0))
    def _():
        for ref in (ext_ref, carry_ref, u_ref, z_ref, k_ref, s1_ref, s2_ref, mu_ref, rstd_ref):
            ref[...] = jnp.zeros(ref.shape, ref.dtype)

    def conv_block(b):
        first = jp == 0
        nv = (SHIFT_ROWS + HALO_B) // SUBLANES
        x3 = ext_ref[b * SHIFT_ROWS:(b + 1) * SHIFT_ROWS + HALO_B, :].reshape(nv, SUBLANES, cw)
        sh_ref[0] = x3
        sub = lax.broadcasted_iota(jnp.int32, x3.shape, 1)
        for r in range(1, SUBLANES):
            rot = pltpu.roll(x3, r, 1)
            prev = jnp.concatenate([rot[:1], rot[:-1]], axis=0)
            sh_ref[r] = jnp.where(sub < r, prev, rot)
        nvb = ACC_ROWS // SUBLANES
        for rb in range(SHIFT_ROWS // ACC_ROWS):
            accs = []
            n_a = HALO_B // SUBLANES
            for lc in range(cw // LANES):
                lanes = slice(lc * LANES, (lc + 1) * LANES)
                acc = [jnp.zeros((1, SUBLANES, LANES), _f32) + cb_ref[:, lanes] for _ in range(nvb)]
                for r in range(SUBLANES):
                    for v in range(nvb + n_a - 1):
                        t = sh_ref[r, rb * nvb + 1 + v:rb * nvb + 2 + v, :, lanes]
                        for a in range(n_a):
                            d, i = SUBLANES * a + r, v - (n_a - 1) + a
                            if d < CONV_B and 0 <= i < nvb:
                                acc[i] = acc[i] + cw_ref[CONV_B - 1 - d:CONV_B - d, lanes] * t
                accs.append(jnp.concatenate(acc, axis=0))
            row0 = b * SHIFT_ROWS + rb * ACC_ROWS
            rows = slice(row0, row0 + ACC_ROWS)
            u_ref[jp, row0 // SUBLANES:row0 // SUBLANES + nvb] = jnp.concatenate(accs, axis=-1)
            rs = jnp.sum(sum(accs), axis=-1, keepdims=True).reshape(ACC_ROWS, 1)
            k = jnp.where(first, rs * (1.0 / cw), k_ref[rows])
            k3 = k.reshape(nvb, SUBLANES, 1)
            sq = sum((acc - k3) * (acc - k3) for acc in accs)
            s1 = jnp.where(first, 0.0, s1_ref[rows]) + (rs - cw * k)
            s2 = (jnp.where(first, 0.0, s2_ref[rows])
                  + jnp.sum(sq, axis=-1, keepdims=True).reshape(ACC_ROWS, 1))
            k_ref[rows] = k
            s1_ref[rows] = s1
            s2_ref[rows] = s2

    def conv_pending():
        for b in range(ts // SHIFT_ROWS):
            conv_block(b)

    def finish_chunk():
        m1 = s1_ref[...] * (1.0 / width)
        var = s2_ref[...] * (1.0 / width) - m1 * m1
        mu = jnp.where(j == 0, k_ref[...] + m1, mu_ref[...])
        rstd = jnp.where(j == 0, lax.rsqrt(var + EPS), rstd_ref[...])
        mu_ref[...] = mu
        rstd_ref[...] = rstd
        u = u_ref[j].reshape(ts, cw)
        ln = (u - mu) * rstd * lg_ref[...] + lb_ref[...]
        y_ref[0] = (jax.nn.silu(ln) * z_ref[j]).astype(_bf16)

    def mxu_phase():
        h = h_ref[0]
        z_ref[j] = jax.nn.silu(_dot(h, wz_ref[...])).astype(_bf16)
        gate = jax.nn.sigmoid(_dot(h, wg_ref[...]))
        glu = _dot(h, wv_ref[...]) * gate
        seq_start = lax.rem(tt, tiles_per_seq) == 0
        ext_ref[0:HALO_B] = jnp.where(seq_start, 0.0, carry_ref[j])
        ext_ref[HALO_B:] = glu
        carry_ref[j] = glu[ts - HALO_B:]

    @pl.when(tt < n_tiles)
    def _():
        conv_pending()
        finish_chunk()
        mxu_phase()

    @pl.when(tt == n_tiles)
    def _():
        pl.when(j == 0)(conv_pending)
        finish_chunk()


def _mixer_b(h, w_in, conv_w, conv_b, ln_g, ln_b, col0, w_b, ts, cw):
    b, s, d = h.shape
    nj = w_b // cw
    j0 = col0 // cw
    tiles_per_seq = s // ts
    n_tiles = b * tiles_per_seq

    def tile(tt):
        return tt // tiles_per_seq, lax.rem(tt, tiles_per_seq)

    def h_map(tt, j):
        return (*tile(jnp.minimum(tt, n_tiles - 1)), 0)

    def y_map(tt, j):
        return (*tile(jnp.maximum(tt - 1, 0)), j * jnp.minimum(tt, 1))

    def wspec(k):
        return pl.BlockSpec((d, cw), lambda tt, j, k=k: (0, j0 + k * nj + j))

    def pending(tt, j):
        return (0, lax.rem(j + nj - 1, nj))

    stat = pltpu.VMEM((ts, 1), _f32)
    return pl.pallas_call(
        functools.partial(_mixer_b_kernel, ts=ts, cw=cw, nj=nj, n_tiles=n_tiles,
                          tiles_per_seq=tiles_per_seq),
        grid=(n_tiles + 1, nj),
        in_specs=[pl.BlockSpec((1, ts, d), h_map),
                  wspec(0), wspec(1), wspec(2),
                  pl.BlockSpec((CONV_B, cw), pending),
                  pl.BlockSpec((1, cw), pending),
                  pl.BlockSpec((1, cw), lambda tt, j: (0, j)),
                  pl.BlockSpec((1, cw), lambda tt, j: (0, j))],
        out_specs=pl.BlockSpec((1, ts, cw), y_map),
        out_shape=jax.ShapeDtypeStruct((b, s, w_b), _bf16),
        scratch_shapes=[pltpu.VMEM((HALO_B + ts, cw), _f32),
                        pltpu.VMEM((SUBLANES, (SHIFT_ROWS + HALO_B) // SUBLANES, SUBLANES, cw), _f32),
                        pltpu.VMEM((nj, HALO_B, cw), _f32),
                        pltpu.VMEM((nj, ts // SUBLANES, SUBLANES, cw), _f32),
                        pltpu.VMEM((nj, ts, cw), _bf16),
                        stat, stat, stat, stat, stat],
        compiler_params=_params(("arbitrary", "arbitrary")),
        name="mixer_b",
    )(h, w_in, w_in, w_in, conv_w, conv_b, ln_g, ln_b)


def _outproj_kernel(ya_ref, yb_ref, w_ref, x_ref, gate_ref, fg_ref, o_ref, *, tn):
    ts, d = o_ref.shape[1:]
    w_a = ya_ref.shape[-1]
    ya = ya_ref[0]
    yb = yb_ref[0]
    ss = jnp.zeros((ts, 1), _f32)
    for k in range(d // tn):
        cols = slice(k * tn, (k + 1) * tn)
        delta = _dot(ya, w_ref[0:w_a, cols]) + _dot(yb, w_ref[w_a:, cols])
        r = x_ref[0, :, cols] + gate_ref[0, :, cols] * delta
        o_ref[0, :, cols] = r
        ss = ss + jnp.sum(r * r, axis=-1, keepdims=True)
    rstd = lax.rsqrt(ss * (1.0 / d) + EPS)
    o_ref[0] = o_ref[0] * rstd * fg_ref[...]


def _outproj(y_a, y_b, w_out, x, gate, final_g, ts, tn):
    b, s, d = x.shape
    w_a = y_a.shape[-1]
    w_b = y_b.shape[-1]
    return pl.pallas_call(
        functools.partial(_outproj_kernel, tn=tn),
        grid=(b, s // ts),
        in_specs=[pl.BlockSpec((1, ts, w_a), lambda i, t: (i, t, 0)),
                  pl.BlockSpec((1, ts, w_b), lambda i, t: (i, t, 0)),
                  pl.BlockSpec((w_a + w_b, d), lambda i, t: (0, 0), pipeline_mode=pl.Buffered(1)),
                  pl.BlockSpec((1, ts, d), lambda i, t: (i, t, 0)),
                  pl.BlockSpec((1, 1, d), lambda i, t: (i, 0, 0)),
                  pl.BlockSpec((1, d), lambda i, t: (0, 0))],
        out_specs=pl.BlockSpec((1, ts, d), lambda i, t: (i, t, 0)),
        out_shape=jax.ShapeDtypeStruct((b, s, d), _f32),
        compiler_params=_params(("arbitrary", "arbitrary")),
        name="outproj",
    )(y_a, y_b, w_out, x, gate, final_g)


def kernel(x, c, norm_g, w_ada, b_ada, w_in, conv_a_w, conv_b_w, conv_b_b, ln_b_g,
           ln_b_b, w_out, final_g):
    batch, seq, d_model = x.shape
    depth = w_ada.shape[0]
    w_a = conv_a_w.shape[-1]
    w_b = conv_b_w.shape[-1]
    assert depth == 1, "the final RMSNorm is fused into the single layer's output projection"
    assert batch <= SUBLANES
    c_pad = jnp.zeros((SUBLANES, d_model), _f32).at[:batch].set(c)
    mod = _ada(c_pad, w_ada[0], b_ada[0][None, :], tn=512)[:batch]
    mod3 = mod.reshape(batch, 3, d_model)
    gate = mod3[:, 2:3, :]
    w_in_bf = w_in[0].astype(_bf16)
    w_out_bf = w_out[0].astype(_bf16)
    h = _prenorm(x, mod3, norm_g[0][None, :], ts=512)
    y_a = _mixer_a(h, w_in_bf, conv_a_w[0], w_a, ts=1024, cw=256)
    y_b = _mixer_b(h, w_in_bf, conv_b_w[0], conv_b_b[0][None, :], ln_b_g[0][None, :],
                   ln_b_b[0][None, :], col0=4 * w_a, w_b=w_b, ts=1024, cw=256)
    return _outproj(y_a, y_b, w_out_bf, x, gate, final_g[None, :], ts=256, tn=512)
```
